```python
import math
import jax, jax.numpy as jnp
from jax import lax
import numpy as np

D_MODEL = 1024
BATCH = 2
SEQ = 8192
DEPTH = 1
DEC_BATCH = 128
DEC_SEQ = 4
PAST_LEN = 8192
PAGE_SIZE = 128

HEAD_DIM = 64
MIX_WIDTH = D_MODEL
A_WIDTH = MIX_WIDTH // 2
B_WIDTH = MIX_WIDTH - A_WIDTH
N_HEADS_A = A_WIDTH // HEAD_DIM
N_KV_A = max(1, N_HEADS_A // 4)
N_HEADS_B = B_WIDTH // HEAD_DIM
N_IDX_HEADS = 8
IDX_DIM = 64
TOPK_MAX = 256
D_FF = 4 * D_MODEL
ROPE_THETA = 10000.0
Q_BLOCK = 128
LN_EPS = 1e-5
DEEPNORM_ALPHA = (2.0 * DEPTH) ** 0.25
DEEPNORM_BETA = (8.0 * DEPTH) ** -0.25
SPLIT_NAMES = ("q_a", "k_a", "v_a", "q_idx", "k_idx", "w_idx", "q_b", "k_b", "v_b")
SPLIT_SIZES = (N_HEADS_A * HEAD_DIM, N_KV_A * HEAD_DIM, N_KV_A * HEAD_DIM,
               N_IDX_HEADS * IDX_DIM, IDX_DIM, N_IDX_HEADS,
               N_HEADS_B * HEAD_DIM, N_HEADS_B * HEAD_DIM, N_HEADS_B * HEAD_DIM)
IN_COLS = sum(SPLIT_SIZES)

kernel_name = "hybrid_dsa_stickbreak_deepnorm_adaln_step"


def _layernorm(x, g, b):
    xf = x.astype(jnp.float32)
    mu = jnp.mean(xf, axis=-1, keepdims=True)
    var = jnp.mean(jnp.square(xf - mu), axis=-1, keepdims=True)
    return ((xf - mu) * lax.rsqrt(var + LN_EPS) * g + b).astype(x.dtype)


def _rope(x, pos):
    d = x.shape[-1]
    half = d // 2
    inv = 1.0 / (ROPE_THETA ** (jnp.arange(half, dtype=jnp.float32) * (2.0 / d)))
    ang = pos.astype(jnp.float32)[:, None] * inv[None, :]
    cos = jnp.cos(ang)[:, None, :]
    sin = jnp.sin(ang)[:, None, :]
    xf = x.astype(jnp.float32)
    x1, x2 = xf[..., :half], xf[..., half:]
    return jnp.concatenate([x1 * cos - x2 * sin, x2 * cos + x1 * sin], axis=-1).astype(x.dtype)


def _modulation(c, w_mod_l, b_mod_l):
    m = (c @ w_mod_l + b_mod_l)[:, None, :]
    return jnp.split(m, 6, axis=-1)


def _project(h, w_in_l, pos):
    B, S = h.shape[:2]
    offs = np.cumsum(SPLIT_SIZES)[:-1].tolist()
    q_a, k_a, v_a, q_i, k_i, w_i, q_b, k_b, v_b = jnp.split(h @ w_in_l, offs, axis=-1)
    q_a = _rope(q_a.reshape(B, S, N_HEADS_A, HEAD_DIM), pos)
    k_a = _rope(k_a.reshape(B, S, N_KV_A, HEAD_DIM), pos)
    v_a = v_a.reshape(B, S, N_KV_A, HEAD_DIM)
    q_i = _rope(q_i.reshape(B, S, N_IDX_HEADS, IDX_DIM), pos)
    k_i = _rope(k_i[:, :, None, :], pos)[:, :, 0, :]
    w_i = w_i * (N_IDX_HEADS ** -0.5)
    q_b = q_b.reshape(B, S, N_HEADS_B, HEAD_DIM)
    k_b = k_b.reshape(B, S, N_HEADS_B, HEAD_DIM)
    v_b = v_b.reshape(B, S, N_HEADS_B, HEAD_DIM)
    return (q_a, k_a, v_a, q_i, k_i, w_i, q_b, k_b, v_b)


def _indexer_topk(q_idx, w_idx, k_idx, q_pos, k_pos, topk):
    dots = jnp.einsum('bthd,bsd->bths', q_idx.astype(jnp.float32),
                      k_idx.astype(jnp.float32)) * (IDX_DIM ** -0.5)
    score = jnp.einsum('bth,bths->bts', w_idx.astype(jnp.float32), jax.nn.relu(dots))
    causal = k_pos[None, :] <= q_pos[:, None]
    score = jnp.where(causal[None], score, -jnp.inf)
    vals, idx = lax.top_k(score, topk)
    return idx, jnp.isfinite(vals)


def _sparse_attend(q, k_sel, v_sel, valid):
    n_kv = k_sel.shape[-2]
    g = q.shape[-2] // n_kv
    qg = q.reshape(q.shape[:-2] + (n_kv, g, q.shape[-1])).astype(jnp.float32)
    s = jnp.einsum('...ngd,...jnd->...ngj', qg, k_sel.astype(jnp.float32)) * (HEAD_DIM ** -0.5)
    s = jnp.where(valid[..., None, None, :], s, -jnp.inf)
    p = jax.nn.softmax(s, axis=-1)
    o = jnp.einsum('...ngj,...jnd->...ngd', p, v_sel.astype(jnp.float32))
    return o.reshape(q.shape).astype(q.dtype)


def _stick_breaking_weights(z, mask):
    log_keep = jnp.where(mask, jax.nn.log_sigmoid(-z), 0.0)
    after = lax.cumsum(log_keep, axis=z.ndim - 1, reverse=True) - log_keep
    return jnp.where(mask, jnp.exp(jax.nn.log_sigmoid(z) + after), 0.0)


def _prompt_mixers(q_a, k_a, v_a, q_i, k_i, w_i, q_b, k_b, v_b):
    B, S = q_a.shape[:2]
    nb = S // Q_BLOCK
    pos = jnp.arange(S, dtype=jnp.int32)
    topk = min(TOPK_MAX, S // 4)
    gather = jax.vmap(lambda t, i: t[i])

    def blockify(t):
        return jnp.moveaxis(t.reshape((B, nb, Q_BLOCK) + t.shape[2:]), 1, 0)

    def one_block(args):
        qa, qi, wi, qb, qpos = args
        idx, valid = _indexer_topk(qi, wi, k_i, qpos, pos, topk)
        oa = _sparse_attend(qa, gather(k_a, idx), gather(v_a, idx), valid)
        z = jnp.einsum('bthd,bshd->bhts', qb.astype(jnp.float32),
                       k_b.astype(jnp.float32)) * (HEAD_DIM ** -0.5)
        a = _stick_breaking_weights(z, pos[None, :] < qpos[:, None])
        ob = jnp.einsum('bhts,bshd->bthd', a, v_b.astype(jnp.float32)).astype(qb.dtype)
        return oa, ob

    oa, ob = lax.map(one_block, (blockify(q_a), blockify(q_i), blockify(w_i), blockify(q_b),
                                 pos.reshape(nb, Q_BLOCK)))
    oa = jnp.moveaxis(oa, 0, 1).reshape(B, S, N_HEADS_A, HEAD_DIM)
    ob = jnp.moveaxis(ob, 0, 1).reshape(B, S, N_HEADS_B, HEAD_DIM)
    return oa, ob


def _sample_mixers(q_a, k_a, v_a, q_i, k_i, w_i, q_b, k_b, v_b,
                   pool_k_a, pool_v_a, pool_k_idx, pool_k_b, pool_v_b, page_table):
    DB, T = q_a.shape[:2]
    P = page_table.shape[1] * PAGE_SIZE
    L = P + T
    k_pos = jnp.arange(L, dtype=jnp.int32)
    q_pos = P + jnp.arange(T, dtype=jnp.int32)

    def gather_past(pool):
        return pool[page_table].reshape((DB, P) + pool.shape[2:])

    k_idx_all = jnp.concatenate([gather_past(pool_k_idx), k_i], axis=1)
    idx, valid = _indexer_topk(q_i, w_i, k_idx_all, q_pos, k_pos, min(TOPK_MAX, L // 4))
    bidx = jnp.arange(DB, dtype=jnp.int32)[:, None, None]
    past_i = jnp.minimum(idx, P - 1)
    phys = page_table[bidx, past_i // PAGE_SIZE]
    off = past_i % PAGE_SIZE
    new_i = jnp.clip(idx - P, 0, T - 1)
    is_new = (idx >= P)[..., None, None]

    def select(pool, new):
        return jnp.where(is_new, new[bidx, new_i], pool[phys, off])

    oa = _sparse_attend(q_a, select(pool_k_a, k_a), select(pool_v_a, v_a), valid)

    qbf = q_b.astype(jnp.float32)
    scale = HEAD_DIM ** -0.5
    z_past = jnp.einsum('bthd,bshd->bhts', qbf, gather_past(pool_k_b).astype(jnp.float32)) * scale
    z_new = jnp.einsum('bthd,bshd->bhts', qbf, k_b.astype(jnp.float32)) * scale
    a = _stick_breaking_weights(jnp.concatenate([z_past, z_new], axis=-1),
                                k_pos[None, :] < q_pos[:, None])
    ob = (jnp.einsum('bhts,bshd->bthd', a[..., :P], gather_past(pool_v_b).astype(jnp.float32))
          + jnp.einsum('bhts,bshd->bthd', a[..., P:], v_b.astype(jnp.float32))).astype(q_b.dtype)
    return oa, ob


def _layer_tail(x, oa, ob, mod, w_o_l, ln1_g_l, ln1_b_l, w_up_l, w_down_l, ln2_g_l, ln2_b_l):
    _, _, gate_m, shift_f, scale_f, gate_f = mod
    B, S = x.shape[:2]
    mix = jnp.concatenate([oa.reshape(B, S, A_WIDTH), ob.reshape(B, S, B_WIDTH)], axis=-1) @ w_o_l
    x = _layernorm(DEEPNORM_ALPHA * x + gate_m * mix, ln1_g_l, ln1_b_l)
    h = x * (1 + scale_f) + shift_f
    f = jnp.square(jax.nn.relu(h @ w_up_l)) @ w_down_l
    return _layernorm(DEEPNORM_ALPHA * x + gate_f * f, ln2_g_l, ln2_b_l)


def setup_inputs(seed: int = 0) -> dict:
    key = jax.random.key(seed)
    ks = jax.random.split(key, 24)
    f32 = jnp.float32
    n_pages = PAST_LEN // PAGE_SIZE
    n_used = DEC_BATCH * n_pages
    n_phys = n_used + max(1, n_used // 4)

    def nrm(k, shape, s=1.0):
        return jax.random.normal(k, shape, f32) * s

    col_scale = jnp.concatenate([
        jnp.full((n,), DEEPNORM_BETA if name in ("v_a", "v_b") else 1.0, f32)
        for name, n in zip(SPLIT_NAMES, SPLIT_SIZES)])
    page_table = jax.random.permutation(ks[7], n_phys)[:n_used].reshape(DEC_BATCH, n_pages).astype(jnp.int32)
    return {
        "x_prompt": nrm(ks[0], (BATCH, SEQ, D_MODEL)),
        "x_sample": nrm(ks[1], (DEC_BATCH, DEC_SEQ, D_MODEL)),
        "cache_k_a": nrm(ks[2], (DEPTH, n_phys, PAGE_SIZE, N_KV_A, HEAD_DIM)),
        "cache_v_a": nrm(ks[3], (DEPTH, n_phys, PAGE_SIZE, N_KV_A, HEAD_DIM), DEEPNORM_BETA),
        "cache_k_idx": nrm(ks[4], (DEPTH, n_phys, PAGE_SIZE, IDX_DIM)),
        "cache_k_b": nrm(ks[5], (DEPTH, n_phys, PAGE_SIZE, N_HEADS_B, HEAD_DIM)),
        "cache_v_b": nrm(ks[6], (DEPTH, n_phys, PAGE_SIZE, N_HEADS_B, HEAD_DIM), DEEPNORM_BETA),
        "page_table": page_table,
        "c_prompt": nrm(ks[8], (BATCH, D_MODEL)),
        "c_sample": nrm(ks[9], (DEC_BATCH, D_MODEL)),
        "w_mod": nrm(ks[10], (DEPTH, D_MODEL, 6 * D_MODEL), 0.5 * D_MODEL ** -0.5),
        "b_mod": nrm(ks[11], (DEPTH, 6 * D_MODEL), 0.01),
        "w_in": nrm(ks[12], (DEPTH, D_MODEL, IN_COLS), D_MODEL ** -0.5) * col_scale,
        "w_o": nrm(ks[13], (DEPTH, MIX_WIDTH, D_MODEL), MIX_WIDTH ** -0.5 * DEEPNORM_BETA),
        "ln1_g": 1.0 + nrm(ks[14], (DEPTH, D_MODEL), 0.02),
        "ln1_b": nrm(ks[15], (DEPTH, D_MODEL), 0.02),
        "w_up": nrm(ks[16], (DEPTH, D_MODEL, D_FF), D_MODEL ** -0.5),
        "w_down": nrm(ks[17], (DEPTH, D_FF, D_MODEL), D_FF ** -0.5 * DEEPNORM_BETA),
        "ln2_g": 1.0 + nrm(ks[18], (DEPTH, D_MODEL), 0.02),
        "ln2_b": nrm(ks[19], (DEPTH, D_MODEL), 0.02),
    }


def reference(x_prompt, x_sample, cache_k_a, cache_v_a, cache_k_idx, cache_k_b, cache_v_b,
              page_table, c_prompt, c_sample, w_mod, b_mod, w_in, w_o, ln1_g, ln1_b,
              w_up, w_down, ln2_g, ln2_b):
    pos_p = jnp.arange(x_prompt.shape[1], dtype=jnp.int32)
    pos_s = page_table.shape[1] * PAGE_SIZE + jnp.arange(x_sample.shape[1], dtype=jnp.int32)
    yp, ys = x_prompt, x_sample
    rows_p, rows_s = [], []
    for l in range(DEPTH):
        mod_p = _modulation(c_prompt, w_mod[l], b_mod[l])
        proj_p = _project(yp * (1 + mod_p[1]) + mod_p[0], w_in[l], pos_p)
        oa_p, ob_p = _prompt_mixers(*proj_p)
        yp = _layer_tail(yp, oa_p, ob_p, mod_p, w_o[l], ln1_g[l], ln1_b[l],
                         w_up[l], w_down[l], ln2_g[l], ln2_b[l])
        rows_p.append((proj_p[1], proj_p[2], proj_p[4], proj_p[7], proj_p[8]))
        mod_s = _modulation(c_sample, w_mod[l], b_mod[l])
        proj_s = _project(ys * (1 + mod_s[1]) + mod_s[0], w_in[l], pos_s)
        oa_s, ob_s = _sample_mixers(*proj_s, cache_k_a[l], cache_v_a[l], cache_k_idx[l],
                                    cache_k_b[l], cache_v_b[l], page_table)
        ys = _layer_tail(ys, oa_s, ob_s, mod_s, w_o[l], ln1_g[l], ln1_b[l],
                         w_up[l], w_down[l], ln2_g[l], ln2_b[l])
        rows_s.append((proj_s[1], proj_s[2], proj_s[4], proj_s[7], proj_s[8]))

    k_a_prompt = jnp.stack([r[0] for r in rows_p])
    v_a_prompt = jnp.stack([r[1] for r in rows_p])
    k_idx_prompt = jnp.stack([r[2] for r in rows_p])
    k_b_prompt = jnp.stack([r[3] for r in rows_p])
    v_b_prompt = jnp.stack([r[4] for r in rows_p])
    k_a_sample = jnp.stack([r[0] for r in rows_s])
    v_a_sample = jnp.stack([r[1] for r in rows_s])
    k_idx_sample = jnp.stack([r[2] for r in rows_s])
    k_b_sample = jnp.stack([r[3] for r in rows_s])
    v_b_sample = jnp.stack([r[4] for r in rows_s])
    return (yp, ys, k_a_prompt, v_a_prompt, k_idx_prompt, k_b_prompt, v_b_prompt,
            k_a_sample, v_a_sample, k_idx_sample, k_b_sample, v_b_sample)
```

```python
import functools

import numpy as np
import jax
import jax.numpy as jnp
from jax import lax
from jax.experimental import pallas as pl
from jax.experimental.pallas import tpu as pltpu

F32 = jnp.float32
BF16 = jnp.bfloat16
I32 = jnp.int32

HEAD_DIM = 64
N_HEADS_A = 8
N_KV_A = 2
N_HEADS_B = 8
N_IDX_HEADS = 8
IDX_DIM = 64
TOPK_MAX = 256
PAGE_SIZE = 128
ROPE_THETA = 10000.0
LN_EPS = 1e-5
LANES = 128
HALF = HEAD_DIM // 2
QK_SCALE = HEAD_DIM ** -0.5
IDX_SCALE = (IDX_DIM ** -0.5) * (N_IDX_HEADS ** -0.5)
INT_MIN = -(2 ** 31)
NEG_BIG = -1e30
SB_EXIT = -110.0
VMEM_LIMIT = 56 * 1024 * 1024

_W_QA = N_HEADS_A * LANES
_W_KA = N_KV_A * HEAD_DIM
_W_QI = N_IDX_HEADS * IDX_DIM
_W_B = N_HEADS_B * HEAD_DIM
_OFF_QA = 0
_OFF_KA = _OFF_QA + _W_QA
_OFF_VA = _OFF_KA + _W_KA
_OFF_QI = _OFF_VA + _W_KA
_OFF_KW = _OFF_QI + _W_QI
_OFF_QB = _OFF_KW + LANES
_OFF_KB = _OFF_QB + _W_B
_OFF_VB = _OFF_KB + _W_B
_W_TOT = _OFF_VB + _W_B


def _nt(a, b):
    return lax.dot_general(a, b, (((1,), (1,)), ((), ())), preferred_element_type=F32)


def _params(sem):
    return pltpu.CompilerParams(dimension_semantics=sem, vmem_limit_bytes=VMEM_LIMIT)


def _mod_kernel(c_ref, w_ref, b_ref, o_ref):
    o_ref[...] = jnp.dot(c_ref[...], w_ref[...], preferred_element_type=F32,
                         precision=lax.Precision.HIGHEST) + b_ref[...]


def _modulation(c, w_mod, b_mod):
    r, d = c.shape
    n = w_mod.shape[1]
    tn = 1536 if n % 1536 == 0 else n
    return pl.pallas_call(
        _mod_kernel,
        grid=(n // tn,),
        in_specs=[pl.BlockSpec((r, d), lambda j: (0, 0)),
                  pl.BlockSpec((d, tn), lambda j: (0, j)),
                  pl.BlockSpec((1, tn), lambda j: (0, j))],
        out_specs=pl.BlockSpec((r, tn), lambda j: (0, j)),
        out_shape=jax.ShapeDtypeStruct((r, n), F32),
        compiler_params=_params(("arbitrary",)),
        name="modulation",
    )(c, w_mod, b_mod.reshape(1, n))


def _rope(x, cos, sin_signed):
    w = x.shape[1]
    lane = lax.broadcasted_iota(I32, x.shape, 1)
    partner = jnp.where(lane % HEAD_DIM < HALF, pltpu.roll(x, w - HALF, 1), pltpu.roll(x, HALF, 1))
    reps = w // LANES
    if reps > 1:
        cos = jnp.concatenate([cos] * reps, axis=1)
        sin_signed = jnp.concatenate([sin_signed] * reps, axis=1)
    return x * cos + partner * sin_signed


def _proj_kernel(x_ref, shift_ref, scale_ref, cos_ref, sin_ref, w_ref,
                 qa_ref, ka_ref, va_ref, qi_ref, ki_ref, wi_ref, qb_ref, kb_ref, vb_ref,
                 kab_ref, vab_ref, ki2_ref, kbb_ref, vbb_ref):
    h = (x_ref[0] * (1.0 + scale_ref[0]) + shift_ref[0]).astype(BF16)
    cos = cos_ref[...]
    sin = sin_ref[...]

    def mm(off, width):
        return jnp.dot(h, w_ref[:, off:off + width], preferred_element_type=F32)

    for hh in range(N_HEADS_A):
        q = _rope(mm(_OFF_QA + hh * LANES, LANES), cos, sin) * QK_SCALE
        qa_ref[0, :, hh * LANES:(hh + 1) * LANES] = q.astype(BF16)
    ka = _rope(mm(_OFF_KA, _W_KA), cos, sin)
    ka_ref[0] = ka
    kab_ref[0] = ka.astype(BF16)
    va = mm(_OFF_VA, _W_KA)
    va_ref[0] = va
    vab_ref[0] = va.astype(BF16)
    qi_ref[0] = _rope(mm(_OFF_QI, _W_QI), cos, sin).astype(BF16)
    kw = mm(_OFF_KW, LANES)
    lane = lax.broadcasted_iota(I32, kw.shape, 1)
    ki = jnp.where(lane < IDX_DIM, _rope(kw, cos, sin), 0.0)
    ki_ref[0] = ki[:, :IDX_DIM]
    wi_ref[0] = kw
    kib = ki.astype(BF16)
    ki2_ref[0, :, :LANES] = kib
    ki2_ref[0, :, LANES:] = pltpu.roll(ki, IDX_DIM, 1).astype(BF16)
    qb_ref[0] = (mm(_OFF_QB, _W_B) * QK_SCALE).astype(BF16)
    kb = mm(_OFF_KB, _W_B)
    kb_ref[0] = kb
    kbb_ref[0] = kb.astype(BF16)
    vb = mm(_OFF_VB, _W_B)
    vb_ref[0] = vb
    vbb_ref[0] = vb.astype(BF16)


def _project(x, shift, scale, cos, sin, w_pad):
    g, r, d = x.shape
    tm = min(512, r)
    rm = shift.shape[1]
    mod_blk = (1, 1, d) if rm == 1 else (1, tm, d)
    mod_map = (lambda b, i: (b, 0, 0)) if rm == 1 else (lambda b, i: (b, i, 0))

    def out(width, dtype):
        return (pl.BlockSpec((1, tm, width), lambda b, i: (b, i, 0)),
                jax.ShapeDtypeStruct((g, r, width), dtype))

    outs = [out(_W_QA, BF16), out(_W_KA, F32), out(_W_KA, F32), out(_W_QI, BF16),
            out(IDX_DIM, F32), out(LANES, F32), out(_W_B, BF16), out(_W_B, F32), out(_W_B, F32),
            out(_W_KA, BF16), out(_W_KA, BF16), out(2 * LANES, BF16), out(_W_B, BF16), out(_W_B, BF16)]
    return pl.pallas_call(
        _proj_kernel,
        grid=(g, r // tm),
        in_specs=[pl.BlockSpec((1, tm, d), lambda b, i: (b, i, 0)),
                  pl.BlockSpec(mod_blk, mod_map),
                  pl.BlockSpec(mod_blk, mod_map),
                  pl.BlockSpec((tm, LANES), lambda b, i: (i, 0)),
                  pl.BlockSpec((tm, LANES), lambda b, i: (i, 0)),
                  pl.BlockSpec((d, _W_TOT), lambda b, i: (0, 0))],
        out_specs=[o[0] for o in outs],
        out_shape=[o[1] for o in outs],
        compiler_params=_params(("arbitrary", "arbitrary")),
        name="project",
    )(x, shift, scale, cos, sin, w_pad)


def _sortable(score):
    score = jnp.where(score == 0.0, 0.0, score)
    bits = pltpu.bitcast(score, I32)
    return jnp.where(bits < 0, bits ^ 0x7FFFFFFF, bits)


def _count_rows(keys_ref, nblk, tk, pred):
    rows = keys_ref.shape[0]

    def body(kb, acc):
        col0 = pl.multiple_of(kb * tk, tk)
        hit = jnp.where(pred(keys_ref[:, pl.ds(col0, tk)], col0), 1, 0)
        for c in range(tk // LANES):
            acc = acc + hit[:, c * LANES:(c + 1) * LANES]
        return acc

    acc = lax.fori_loop(0, nblk, body, jnp.zeros((rows, LANES), I32))
    return jnp.sum(acc, axis=1, keepdims=True)


def _topk_threshold(keys_ref, nblk, tk, k, row_ok=True):
    rows = keys_ref.shape[0]
    col_bits = int(keys_ref.shape[1]).bit_length()

    def bit_body(i, ans):
        cand = ans + jnp.left_shift(jnp.int32(1), 31 - i)
        cnt = _count_rows(keys_ref, nblk, tk, lambda blk, col0: blk >= cand)
        return jnp.where(cnt >= k, cand, ans)

    thr = lax.fori_loop(0, 32, bit_body, jnp.full((rows, 1), INT_MIN, I32))
    n_gt = _count_rows(keys_ref, nblk, tk, lambda blk, col0: blk > thr)
    n_eq = _count_rows(keys_ref, nblk, tk, lambda blk, col0: blk == thr)
    need = k - n_gt
    live = thr != INT_MIN
    partial = live & (n_eq != need) & row_ok

    def tie_search():
        def jbody(i, j):
            cand = j + jnp.left_shift(jnp.int32(1), col_bits - 1 - i)

            def pred(blk, col0):
                col = col0 + lax.broadcasted_iota(I32, blk.shape, 1)
                return (blk == thr) & (col < cand)

            cnt = _count_rows(keys_ref, nblk, tk, pred)
            return jnp.where(cnt < need, cand, j)

        return lax.fori_loop(0, col_bits, jbody, jnp.zeros((rows, 1), I32))

    any_partial = jnp.max(jnp.where(partial, 1.0, 0.0)) > 0.5
    jsearch = lax.cond(any_partial, tie_search, lambda: jnp.zeros((rows, 1), I32))
    jmax = jnp.where(live, jnp.where(partial, jsearch, jnp.int32(2 ** 30)), -1)
    return thr, jmax


def _select_bias(key, col0, thr, jmax):
    col = col0 + lax.broadcasted_iota(I32, key.shape, 1)
    sel = (key > thr) | ((key == thr) & (col <= jmax))
    return jnp.where(sel, 0.0, NEG_BIG)


def _head_out_pairs(o_heads):
    pieces = []
    lane = lax.broadcasted_iota(I32, o_heads[0].shape, 1)
    group = N_HEADS_A // N_KV_A
    for j in range(N_HEADS_A // 2):
        a, b = o_heads[2 * j], o_heads[2 * j + 1]
        if (2 * j) // group == 0:
            pieces.append(jnp.where(lane < HEAD_DIM, a, pltpu.roll(b, HEAD_DIM, 1)))
        else:
            pieces.append(jnp.where(lane < HEAD_DIM, pltpu.roll(a, HEAD_DIM, 1), b))
    return jnp.concatenate(pieces, axis=1)


def _prompt_a_kernel(qi_ref, wi_ref, qa_ref, ki2_ref, ka_ref, va_ref, o_ref,
                     keys_ref, m_ref, l_ref, acc_ref, *, tq, tk, topk):
    qblk = pl.program_id(1)
    row = qblk * tq + lax.broadcasted_iota(I32, (tq, tk), 0)
    nblk = (qblk * tq + tq + tk - 1) // tk
    wv = wi_ref[0] * IDX_SCALE

    def score_body(kb, carry):
        col0 = pl.multiple_of(kb * tk, tk)
        kk = ki2_ref[0, pl.ds(col0, tk), :]
        k_lo, k_hi = kk[:, :LANES], kk[:, LANES:]
        score = jnp.zeros((tq, tk), F32)
        for j in range(N_IDX_HEADS // 2):
            qp = qi_ref[0, :, j * LANES:(j + 1) * LANES]
            w0 = wv[:, IDX_DIM + 2 * j:IDX_DIM + 2 * j + 1]
            w1 = wv[:, IDX_DIM + 2 * j + 1:IDX_DIM + 2 * j + 2]
            score = score + w0 * jnp.maximum(_nt(qp, k_lo), 0.0)
            score = score + w1 * jnp.maximum(_nt(qp, k_hi), 0.0)
        col = col0 + lax.broadcasted_iota(I32, (tq, tk), 1)
        keys_ref[:, pl.ds(col0, tk)] = jnp.where(col <= row, _sortable(score), INT_MIN)
        return carry

    lax.fori_loop(0, nblk, score_body, 0)
    thr, jmax = _topk_threshold(keys_ref, nblk, tk, topk)

    m_ref[...] = jnp.full(m_ref.shape, NEG_BIG, F32)
    l_ref[...] = jnp.zeros(l_ref.shape, F32)
    acc_ref[...] = jnp.zeros(acc_ref.shape, F32)

    def attn_body(kb, carry):
        col0 = pl.multiple_of(kb * tk, tk)
        bias = _select_bias(keys_ref[:, pl.ds(col0, tk)], col0, thr, jmax)
        kblk = ka_ref[0, pl.ds(col0, tk), :]
        vblk = va_ref[0, pl.ds(col0, tk), :]
        for hh in range(N_HEADS_A):
            s = _nt(qa_ref[0, :, hh * LANES:(hh + 1) * LANES], kblk) + bias
            m_old = m_ref[hh]
            m_new = jnp.maximum(m_old, jnp.max(s, axis=1, keepdims=True))
            m_safe = jnp.where(m_new < 0.5 * NEG_BIG, 0.0, m_new)
            p = jnp.exp(s - m_safe[:, :1])
            alpha = jnp.exp(m_old - m_new)
            l_ref[hh] = alpha * l_ref[hh] + jnp.sum(p, axis=1, keepdims=True)
            acc_ref[hh] = alpha * acc_ref[hh] + jnp.dot(p.astype(BF16), vblk, preferred_element_type=F32)
            m_ref[hh] = m_new
        return carry

    lax.fori_loop(0, nblk, attn_body, 0)
    o_ref[0] = _head_out_pairs([acc_ref[hh] / l_ref[hh] for hh in range(N_HEADS_A)]).astype(o_ref.dtype)


def _prompt_group_a(qi, wi, qa, ki2, kab, vab):
    b, s, _ = qi.shape
    tq = min(256, s)
    tk = min(512, s)
    topk = min(TOPK_MAX, s // 4)
    kern = functools.partial(_prompt_a_kernel, tq=tq, tk=tk, topk=topk)
    return pl.pallas_call(
        kern,
        grid=(b, s // tq),
        in_specs=[pl.BlockSpec((1, tq, _W_QI), lambda bb, i: (bb, i, 0)),
                  pl.BlockSpec((1, tq, LANES), lambda bb, i: (bb, i, 0)),
                  pl.BlockSpec((1, tq, _W_QA), lambda bb, i: (bb, i, 0)),
                  pl.BlockSpec((1, s, 2 * LANES), lambda bb, i: (bb, 0, 0)),
                  pl.BlockSpec((1, s, LANES), lambda bb, i: (bb, 0, 0)),
                  pl.BlockSpec((1, s, LANES), lambda bb, i: (bb, 0, 0))],
        out_specs=pl.BlockSpec((1, tq, N_HEADS_A * HEAD_DIM), lambda bb, i: (bb, i, 0)),
        out_shape=jax.ShapeDtypeStruct((b, s, N_HEADS_A * HEAD_DIM), BF16),
        scratch_shapes=[pltpu.VMEM((tq, s), I32),
                        pltpu.VMEM((N_HEADS_A, tq, LANES), F32),
                        pltpu.VMEM((N_HEADS_A, tq, LANES), F32),
                        pltpu.VMEM((N_HEADS_A, tq, LANES), F32)],
        compiler_params=_params(("arbitrary", "arbitrary")),
        name="prompt_group_a",
    )(qi, wi, qa, ki2, kab, vab)


def _log_sigmoid(z):
    return jnp.minimum(z, 0.0) - jnp.log(1.0 + jnp.exp(-jnp.abs(z)))


def _split3(x):
    hi = x.astype(BF16)
    r1 = x - hi.astype(F32)
    mid = r1.astype(BF16)
    lo = (r1 - mid.astype(F32)).astype(BF16)
    return hi, mid, lo


def _prompt_b_kernel(q_ref, k_ref, v_ref, o_ref, acc_ref, *, tq, tk):
    qblk = pl.program_id(2)
    q = q_ref[0]
    lane_q = lax.broadcasted_iota(I32, q.shape, 1)
    row = qblk * tq + lax.broadcasted_iota(I32, (tq, tk), 0)
    jj = lax.broadcasted_iota(I32, (tk, tk), 0)
    ss = lax.broadcasted_iota(I32, (tk, tk), 1)
    tri = jnp.where(jj > ss, 1.0, 0.0).astype(BF16)
    kb0 = (qblk * tq + tq - 1) // tk

    for sub in range(2):
        qm = jnp.where((lane_q < HEAD_DIM) == (sub == 0), q, jnp.zeros_like(q))
        acc_ref[sub] = jnp.zeros((tq, LANES), F32)

        def cond(carry):
            kb, _, cmax = carry
            return (kb >= 0) & (cmax > SB_EXIT)

        def body(carry):
            kb, c, _ = carry
            col0 = pl.multiple_of(kb * tk, tk)
            z = _nt(qm, k_ref[0, pl.ds(col0, tk), :])
            valid = (col0 + lax.broadcasted_iota(I32, (tq, tk), 1)) < row
            ls = _log_sigmoid(z)
            lk = jnp.where(valid, ls - z, 0.0)
            hi, mid, lo = _split3(lk)
            after = (jnp.dot(hi, tri, preferred_element_type=F32)
                     + jnp.dot(mid, tri, preferred_element_type=F32)
                     + jnp.dot(lo, tri, preferred_element_type=F32))
            a = jnp.where(valid, jnp.exp(ls + after + c), 0.0)
            acc_ref[sub] += jnp.dot(a.astype(BF16), v_ref[0, pl.ds(col0, tk), :],
                                    preferred_element_type=F32)
            c = c + jnp.sum(lk, axis=1, keepdims=True)
            return kb - 1, c, jnp.max(c)

        lax.while_loop(cond, body, (kb0, jnp.zeros((tq, 1), F32), jnp.float32(0.0)))

    lane = lax.broadcasted_iota(I32, (tq, LANES), 1)
    o_ref[0] = jnp.where(lane < HEAD_DIM, acc_ref[0], acc_ref[1]).astype(o_ref.dtype)


def _prompt_group_b(qb, kbb, vbb):
    b, s, _ = qb.shape
    tq = min(256, s)
    tk = min(256, s)
    npair = N_HEADS_B // 2
    kern = functools.partial(_prompt_b_kernel, tq=tq, tk=tk)
    return pl.pallas_call(
        kern,
        grid=(b, npair, s // tq),
        in_specs=[pl.BlockSpec((1, tq, LANES), lambda bb, j, i: (bb, i, j)),
                  pl.BlockSpec((1, s, LANES), lambda bb, j, i: (bb, 0, j)),
                  pl.BlockSpec((1, s, LANES), lambda bb, j, i: (bb, 0, j))],
        out_specs=pl.BlockSpec((1, tq, LANES), lambda bb, j, i: (bb, i, j)),
        out_shape=jax.ShapeDtypeStruct((b, s, N_HEADS_B * HEAD_DIM), BF16),
        scratch_shapes=[pltpu.VMEM((2, tq, LANES), F32)],
        compiler_params=_params(("arbitrary", "arbitrary", "arbitrary")),
        name="prompt_group_b",
    )(qb, kbb, vbb)


def _sample_a_kernel(pt_ref, qi_ref, w_ref, qa_ref, kin_ref, kan_ref, van_ref,
                     cki_ref, cka_ref, cva_ref, o_ref,
                     ki_buf, ka_buf, va_buf, sems, keys_ref, s_ref, *, n_pages, n_new, topk, chunk):
    b = pl.program_id(0)
    past = n_pages * PAGE_SIZE
    tpad = keys_ref.shape[0]
    rows = qa_ref.shape[1]

    def page_copies(p):
        phys = pt_ref[b, p]
        return (pltpu.make_async_copy(cki_ref.at[phys], ki_buf.at[pl.ds(p * PAGE_SIZE, PAGE_SIZE)], sems.at[0]),
                pltpu.make_async_copy(cka_ref.at[phys], ka_buf.at[pl.ds(p * PAGE_SIZE, PAGE_SIZE)], sems.at[1]),
                pltpu.make_async_copy(cva_ref.at[phys], va_buf.at[pl.ds(p * PAGE_SIZE, PAGE_SIZE)], sems.at[2]))

    def start_page(p, carry):
        for cp in page_copies(p):
            cp.start()
        return carry

    def wait_page(p, carry):
        for cp in page_copies(p):
            cp.wait()
        return carry

    lax.fori_loop(0, n_pages, start_page, 0)
    lax.fori_loop(0, n_pages, wait_page, 0)

    qi = qi_ref[0]
    w = w_ref[0] * IDX_SCALE
    tok = lax.broadcasted_iota(I32, (tpad, LANES), 0)

    def scores_of(kk):
        d = jnp.maximum(_nt(qi, kk), 0.0) * w
        return jnp.sum(d.reshape(tpad, N_IDX_HEADS, kk.shape[0]), axis=1)

    def score_body(c, carry):
        col0 = pl.multiple_of(c * chunk, chunk)
        sc = scores_of(ki_buf[pl.ds(col0, chunk), :].astype(BF16))
        keys_ref[:, pl.ds(col0, chunk)] = _sortable(sc)
        return carry

    lax.fori_loop(0, past // chunk, score_body, 0)
    new_col = lax.broadcasted_iota(I32, (tpad, LANES), 1)
    new_ok = (new_col <= tok) & (new_col < n_new)
    keys_ref[:, past:past + LANES] = jnp.where(new_ok, _sortable(scores_of(kin_ref[0])), INT_MIN)

    width = past + LANES
    sel_tk = max(t for t in range(LANES, 2048 + 1, LANES) if width % t == 0)
    thr, jmax = _topk_threshold(keys_ref, width // sel_tk, sel_tk, topk, row_ok=tok[:, :1] < n_new)

    qa = qa_ref[0]

    def masked_scores(kblk, col0):
        bias = _select_bias(keys_ref[:, pl.ds(col0, kblk.shape[0])], col0, thr, jmax)
        s = _nt(qa, kblk).reshape(tpad, N_HEADS_A, kblk.shape[0]) + bias[:, None, :]
        return s.reshape(rows, kblk.shape[0])

    def s_body(c, m):
        col0 = pl.multiple_of(c * chunk, chunk)
        s = masked_scores(ka_buf[pl.ds(col0, chunk), :].astype(BF16), col0)
        s_ref[:, pl.ds(col0, chunk)] = s
        return jnp.maximum(m, jnp.max(s, axis=1, keepdims=True))

    m = lax.fori_loop(0, past // chunk, s_body, jnp.full((rows, 1), NEG_BIG, F32))
    s_new = masked_scores(kan_ref[0], past)
    m = jnp.maximum(m, jnp.max(s_new, axis=1, keepdims=True))

    p_new = jnp.exp(s_new - m)
    l0 = jnp.sum(p_new, axis=1, keepdims=True)
    acc0 = jnp.dot(p_new.astype(BF16), van_ref[0], preferred_element_type=F32)

    def pv_body(c, carry):
        l, acc = carry
        col0 = pl.multiple_of(c * chunk, chunk)
        p = jnp.exp(s_ref[:, pl.ds(col0, chunk)] - m)
        acc = acc + jnp.dot(p.astype(BF16), va_buf[pl.ds(col0, chunk), :].astype(BF16),
                            preferred_element_type=F32)
        return l + jnp.sum(p, axis=1, keepdims=True), acc

    l, acc = lax.fori_loop(0, past // chunk, pv_body, (l0, acc0))
    o_ref[0] = acc / l


def _sample_group_a(page_table, qi_r, w_r, qa_r, kin, kan, van, cki, cka, cva, n_new):
    db, n_pages = page_table.shape
    past = n_pages * PAGE_SIZE
    rows = qa_r.shape[1]
    tpad = rows // N_HEADS_A
    topk = min(TOPK_MAX, (past + n_new) // 4)
    chunk = min(1024, past)
    kern = functools.partial(_sample_a_kernel, n_pages=n_pages, n_new=n_new, topk=topk, chunk=chunk)
    grid_spec = pltpu.PrefetchScalarGridSpec(
        num_scalar_prefetch=1,
        grid=(db,),
        in_specs=[pl.BlockSpec((1, rows, IDX_DIM), lambda i, pt: (i, 0, 0)),
                  pl.BlockSpec((1, rows, 1), lambda i, pt: (i, 0, 0)),
                  pl.BlockSpec((1, rows, LANES), lambda i, pt: (i, 0, 0)),
                  pl.BlockSpec((1, LANES, IDX_DIM), lambda i, pt: (i, 0, 0)),
                  pl.BlockSpec((1, LANES, LANES), lambda i, pt: (i, 0, 0)),
                  pl.BlockSpec((1, LANES, LANES), lambda i, pt: (i, 0, 0)),
                  pl.BlockSpec(memory_space=pl.ANY),
                  pl.BlockSpec(memory_space=pl.ANY),
                  pl.BlockSpec(memory_space=pl.ANY)],
        out_specs=pl.BlockSpec((1, rows, LANES), lambda i, pt: (i, 0, 0)),
        scratch_shapes=[pltpu.VMEM((past, IDX_DIM), F32),
                        pltpu.VMEM((past, LANES), F32),
                        pltpu.VMEM((past, LANES), F32),
                        pltpu.SemaphoreType.DMA((3,)),
                        pltpu.VMEM((tpad, past + LANES), I32),
                        pltpu.VMEM((rows, past), F32)])
    return pl.pallas_call(
        kern,
        grid_spec=grid_spec,
        out_shape=jax.ShapeDtypeStruct((db, rows, LANES), F32),
        compiler_params=_params(("arbitrary",)),
        name="sample_group_a",
    )(page_table, qi_r, w_r, qa_r, kin, kan, van, cki, cka, cva)


def _sample_b_kernel(pt_ref, q_ref, kn_ref, vn_ref, ck_ref, cv_ref, o_ref,
                     k_buf, v_buf, sems, lk_ref, aft_ref, *, n_pages, n_new):
    b = pl.program_id(0)
    q = q_ref[0]
    nh = N_HEADS_B
    head = lax.broadcasted_iota(I32, (nh, LANES), 0)
    colq = lax.broadcasted_iota(I32, (nh, LANES), 1)
    diag = (colq % nh == head) & (colq < n_new * nh)

    ns = PAGE_SIZE

    def block(kblk, vblk, valid, c):
        z = _nt(kblk, q).reshape(ns, nh, LANES)
        ls = _log_sigmoid(z)
        lk_ref[...] = jnp.where(valid, ls - z, 0.0)

        def scan(i, run):
            s = ns - 1 - i
            aft_ref[s] = run
            return run + lk_ref[s]

        run = lax.fori_loop(0, ns, scan, jnp.zeros((nh, LANES), F32))
        a = jnp.where(valid, jnp.exp(ls + aft_ref[...] + c[None]), 0.0)
        a2 = a.reshape(ns * nh, LANES)
        o_add = jnp.dot(a2.T.astype(BF16), vblk, preferred_element_type=F32)
        return c + run, o_add

    s_new = lax.broadcasted_iota(I32, (ns, nh, LANES), 0)
    valid_new = diag[None] & (s_new < lax.broadcasted_iota(I32, (ns, nh, LANES), 2) // nh)
    c, o = block(kn_ref[0], vn_ref[0], valid_new, jnp.zeros((nh, LANES), F32))

    def cmax_of(c):
        return jnp.max(jnp.where(diag, c, -jnp.inf))

    valid_page = jnp.broadcast_to(diag[None], (PAGE_SIZE, nh, LANES))

    def cond(carry):
        p, _, _, cmax = carry
        return (p >= 0) & (cmax > SB_EXIT)

    def body(carry):
        p, c, o, _ = carry
        phys = pt_ref[b, p]
        ck = pltpu.make_async_copy(ck_ref.at[phys], k_buf, sems.at[0])
        cv = pltpu.make_async_copy(cv_ref.at[phys], v_buf, sems.at[1])
        ck.start()
        cv.start()
        ck.wait()
        cv.wait()
        c, o_add = block(k_buf[...].astype(BF16), v_buf[...].astype(BF16), valid_page, c)
        return p - 1, c, o + o_add, cmax_of(c)

    _, _, o, _ = lax.while_loop(cond, body, (jnp.int32(n_pages - 1), c, o, cmax_of(c)))
    o_ref[0] = o


def _sample_group_b(page_table, qb_r, kbn, vbn, ckb, cvb, n_new):
    db, n_pages = page_table.shape
    rows_new = kbn.shape[1]
    prow = PAGE_SIZE * N_HEADS_B
    kern = functools.partial(_sample_b_kernel, n_pages=n_pages, n_new=n_new)
    grid_spec = pltpu.PrefetchScalarGridSpec(
        num_scalar_prefetch=1,
        grid=(db,),
        in_specs=[pl.BlockSpec((1, LANES, HEAD_DIM), lambda i, pt: (i, 0, 0)),
                  pl.BlockSpec((1, rows_new, HEAD_DIM), lambda i, pt: (i, 0, 0)),
                  pl.BlockSpec((1, rows_new, HEAD_DIM), lambda i, pt: (i, 0, 0)),
                  pl.BlockSpec(memory_space=pl.ANY),
                  pl.BlockSpec(memory_space=pl.ANY)],
        out_specs=pl.BlockSpec((1, LANES, HEAD_DIM), lambda i, pt: (i, 0, 0)),
        scratch_shapes=[pltpu.VMEM((prow, HEAD_DIM), F32),
                        pltpu.VMEM((prow, HEAD_DIM), F32),
                        pltpu.SemaphoreType.DMA((2,)),
                        pltpu.VMEM((PAGE_SIZE, N_HEADS_B, LANES), F32),
                        pltpu.VMEM((PAGE_SIZE, N_HEADS_B, LANES), F32)])
    return pl.pallas_call(
        kern,
        grid_spec=grid_spec,
        out_shape=jax.ShapeDtypeStruct((db, LANES, HEAD_DIM), F32),
        compiler_params=_params(("arbitrary",)),
        name="sample_group_b",
    )(page_table, qb_r, kbn, vbn, ckb, cvb)


def _layernorm(x, g, b):
    mu = jnp.mean(x, axis=-1, keepdims=True)
    xc = x - mu
    var = jnp.mean(xc * xc, axis=-1, keepdims=True)
    return xc * lax.rsqrt(var + LN_EPS) * g + b


def _attn_out_kernel(x_ref, oa_ref, ob_ref, gate_ref, wo_ref, g_ref, b_ref, o_ref, *, alpha):
    half = oa_ref.shape[2]
    mix = (jnp.dot(oa_ref[0], wo_ref[:half, :], preferred_element_type=F32)
           + jnp.dot(ob_ref[0], wo_ref[half:, :], preferred_element_type=F32))
    o_ref[0] = _layernorm(alpha * x_ref[0] + gate_ref[0] * mix, g_ref[...], b_ref[...])


def _attn_out(x, oa, ob, gate, w_o, g, bta, alpha):
    gg, r, d = x.shape
    tm = min(512, r)
    rm = gate.shape[1]
    mod_blk = (1, 1, d) if rm == 1 else (1, tm, d)
    mod_map = (lambda b, i: (b, 0, 0)) if rm == 1 else (lambda b, i: (b, i, 0))
    wa = oa.shape[2]
    return pl.pallas_call(
        functools.partial(_attn_out_kernel, alpha=alpha),
        grid=(gg, r // tm),
        in_specs=[pl.BlockSpec((1, tm, d), lambda b, i: (b, i, 0)),
                  pl.BlockSpec((1, tm, wa), lambda b, i: (b, i, 0)),
                  pl.BlockSpec((1, tm, wa), lambda b, i: (b, i, 0)),
                  pl.BlockSpec(mod_blk, mod_map),
                  pl.BlockSpec((2 * wa, d), lambda b, i: (0, 0)),
                  pl.BlockSpec((1, d), lambda b, i: (0, 0)),
                  pl.BlockSpec((1, d), lambda b, i: (0, 0))],
        out_specs=pl.BlockSpec((1, tm, d), lambda b, i: (b, i, 0)),
        out_shape=jax.ShapeDtypeStruct((gg, r, d), F32),
        compiler_params=_params(("arbitrary", "arbitrary")),
        name="attn_out_ln",
    )(x, oa, ob, gate, w_o, g.reshape(1, d), bta.reshape(1, d))


def _ffn_kernel(x_ref, shift_ref, scale_ref, gate_ref, wu_ref, wd_ref, g_ref, b_ref, o_ref,
                h_ref, acc_ref, *, alpha):
    f = pl.program_id(2)

    @pl.when(f == 0)
    def _():
        h_ref[...] = (x_ref[0] * (1.0 + scale_ref[0]) + shift_ref[0]).astype(BF16)
        acc_ref[...] = jnp.zeros(acc_ref.shape, F32)

    u = jnp.maximum(jnp.dot(h_ref[...], wu_ref[...], preferred_element_type=F32), 0.0)
    acc_ref[...] += jnp.dot((u * u).astype(BF16), wd_ref[...], preferred_element_type=F32)

    @pl.when(f == pl.num_programs(2) - 1)
    def _():
        o_ref[0] = _layernorm(alpha * x_ref[0] + gate_ref[0] * acc_ref[...], g_ref[...], b_ref[...])


def _ffn(x, shift, scale, gate, w_up, w_down, g, bta, alpha):
    gg, r, d = x.shape
    dff = w_up.shape[1]
    tm = min(512, r)
    tf = min(1024, dff)
    rm = gate.shape[1]
    mod_blk = (1, 1, d) if rm == 1 else (1, tm, d)
    mod_map = (lambda b, i, f: (b, 0, 0)) if rm == 1 else (lambda b, i, f: (b, i, 0))
    return pl.pallas_call(
        functools.partial(_ffn_kernel, alpha=alpha),
        grid=(gg, r // tm, dff // tf),
        in_specs=[pl.BlockSpec((1, tm, d), lambda b, i, f: (b, i, 0)),
                  pl.BlockSpec(mod_blk, mod_map),
                  pl.BlockSpec(mod_blk, mod_map),
                  pl.BlockSpec(mod_blk, mod_map),
                  pl.BlockSpec((d, tf), lambda b, i, f: (0, f)),
                  pl.BlockSpec((tf, d), lambda b, i, f: (f, 0)),
                  pl.BlockSpec((1, d), lambda b, i, f: (0, 0)),
                  pl.BlockSpec((1, d), lambda b, i, f: (0, 0))],
        out_specs=pl.BlockSpec((1, tm, d), lambda b, i, f: (b, i, 0)),
        out_shape=jax.ShapeDtypeStruct((gg, r, d), F32),
        scratch_shapes=[pltpu.VMEM((tm, d), BF16), pltpu.VMEM((tm, d), F32)],
        compiler_params=_params(("arbitrary", "arbitrary", "arbitrary")),
        name="ffn_ln",
    )(x, shift, scale, gate, w_up, w_down, g.reshape(1, d), bta.reshape(1, d))


def _rope_tables(pos):
    inv = 1.0 / (ROPE_THETA ** (jnp.arange(HALF, dtype=F32) * (2.0 / HEAD_DIM)))
    ang = pos.astype(F32)[:, None] * inv[None, :]
    cos = jnp.tile(jnp.cos(ang), (1, 2 * LANES // HEAD_DIM))
    sin = jnp.sin(ang)
    sin = jnp.tile(jnp.concatenate([-sin, sin], axis=1), (1, LANES // HEAD_DIM))
    return cos, sin


def _pad_w_in(w_in_l):
    d = w_in_l.shape[0]
    group = N_HEADS_A // N_KV_A
    o_qa, o_ka, o_va = 0, 512, 640
    o_qi, o_ki, o_wi, o_qb, o_kb, o_vb = 768, 1280, 1344, 1352, 1864, 2376
    cols = []
    zero = jnp.zeros((d, HEAD_DIM), w_in_l.dtype)
    for hh in range(N_HEADS_A):
        wq = w_in_l[:, o_qa + hh * HEAD_DIM:o_qa + (hh + 1) * HEAD_DIM]
        cols += [wq, zero] if hh // group == 0 else [zero, wq]
    cols.append(w_in_l[:, o_ka:o_qi])
    cols.append(w_in_l[:, o_qi:o_ki])
    cols.append(w_in_l[:, o_ki:o_qb])
    cols.append(jnp.zeros((d, LANES - IDX_DIM - N_IDX_HEADS), w_in_l.dtype))
    cols.append(w_in_l[:, o_qb:])
    w = jnp.concatenate(cols, axis=1)
    assert w.shape[1] == _W_TOT
    return w.astype(BF16)


def kernel(x_prompt, x_sample, cache_k_a, cache_v_a, cache_k_idx, cache_k_b, cache_v_b, page_table,
           c_prompt, c_sample, w_mod, b_mod, w_in, w_o, ln1_g, ln1_b, w_up, w_down, ln2_g, ln2_b):
    depth = w_in.shape[0]
    alpha = float((2.0 * depth) ** 0.25)
    bsz, seq, d = x_prompt.shape
    db, t_new, _ = x_sample.shape
    n_pages = page_table.shape[1]
    past = n_pages * PAGE_SIZE
    n_phys = cache_k_a.shape[1]
    tpad = 8

    cos_p, sin_p = _rope_tables(jnp.arange(seq, dtype=I32))
    cos_s, sin_s = _rope_tables(past + jnp.arange(t_new, dtype=I32))
    cos_s = jnp.tile(cos_s, (db, 1))
    sin_s = jnp.tile(sin_s, (db, 1))

    yp = x_prompt
    ys = x_sample.reshape(1, db * t_new, d)
    rows_p, rows_s = [], []
    for l in range(depth):
        w_pad = _pad_w_in(w_in[l])
        wo_b, wu_b, wd_b = w_o[l].astype(BF16), w_up[l].astype(BF16), w_down[l].astype(BF16)

        mod_p = _modulation(c_prompt, w_mod[l], b_mod[l])[:, None, :]
        sh_m, sc_m, g_m, sh_f, sc_f, g_f = jnp.split(mod_p, 6, axis=-1)
        (qa, ka, va, qi, ki, wi, qb, kb, vb, kab, vab, ki2, kbb, vbb) = _project(
            yp, sh_m, sc_m, cos_p, sin_p, w_pad)
        oa = _prompt_group_a(qi, wi, qa, ki2, kab, vab)
        ob = _prompt_group_b(qb, kbb, vbb)
        x1 = _attn_out(yp, oa, ob, g_m, wo_b, ln1_g[l], ln1_b[l], alpha)
        yp = _ffn(x1, sh_f, sc_f, g_f, wu_b, wd_b, ln2_g[l], ln2_b[l], alpha)
        rows_p.append((ka.reshape(bsz, seq, N_KV_A, HEAD_DIM), va.reshape(bsz, seq, N_KV_A, HEAD_DIM), ki,
                       kb.reshape(bsz, seq, N_HEADS_B, HEAD_DIM), vb.reshape(bsz, seq, N_HEADS_B, HEAD_DIM)))

        mod_s = _modulation(c_sample, w_mod[l], b_mod[l])
        mod_s = jnp.broadcast_to(mod_s[:, None, :], (db, t_new, 6 * d)).reshape(1, db * t_new, 6 * d)
        sh_m, sc_m, g_m, sh_f, sc_f, g_f = jnp.split(mod_s, 6, axis=-1)
        (qa, ka, va, qi, ki, wi, qb, kb, vb, kab, vab, ki2, kbb, vbb) = _project(
            ys, sh_m, sc_m, cos_s, sin_s, w_pad)

        def pad_tokens(a, n):
            return jnp.pad(a, [(0, 0), (0, n - a.shape[1])] + [(0, 0)] * (a.ndim - 2))

        qi_r = pad_tokens(qi.reshape(db, t_new, N_IDX_HEADS, IDX_DIM), tpad).reshape(db, tpad * N_IDX_HEADS, IDX_DIM)
        w_r = pad_tokens(wi.reshape(db, t_new, LANES)[:, :, IDX_DIM:IDX_DIM + N_IDX_HEADS], tpad)
        w_r = w_r.reshape(db, tpad * N_IDX_HEADS, 1)
        qa_r = pad_tokens(qa.reshape(db, t_new, N_HEADS_A, LANES), tpad).reshape(db, tpad * N_HEADS_A, LANES)
        kin = pad_tokens(ki.reshape(db, t_new, IDX_DIM), LANES).astype(BF16)
        kan = pad_tokens(kab.reshape(db, t_new, LANES), LANES)
        van = pad_tokens(vab.reshape(db, t_new, LANES), LANES)
        oa_r = _sample_group_a(page_table, qi_r, w_r, qa_r, kin, kan, van,
                               cache_k_idx[l], cache_k_a[l].reshape(n_phys, PAGE_SIZE, LANES),
                               cache_v_a[l].reshape(n_phys, PAGE_SIZE, LANES), t_new)
        oa_r = oa_r.reshape(db, tpad, N_KV_A, N_HEADS_A // N_KV_A, N_KV_A, HEAD_DIM)[:, :t_new]
        oa_s = jnp.stack([oa_r[:, :, n, :, n, :] for n in range(N_KV_A)], axis=2)
        oa_s = oa_s.reshape(1, db * t_new, N_HEADS_A * HEAD_DIM).astype(BF16)

        qb_r = pad_tokens(qb.reshape(db, t_new * N_HEADS_B, HEAD_DIM), LANES)
        kbn = pad_tokens(kbb.reshape(db, t_new * N_HEADS_B, HEAD_DIM), PAGE_SIZE * N_HEADS_B)
        vbn = pad_tokens(vbb.reshape(db, t_new * N_HEADS_B, HEAD_DIM), PAGE_SIZE * N_HEADS_B)
        ob_r = _sample_group_b(page_table, qb_r, kbn, vbn,
                               cache_k_b[l].reshape(n_phys, PAGE_SIZE * N_HEADS_B, HEAD_DIM),
                               cache_v_b[l].reshape(n_phys, PAGE_SIZE * N_HEADS_B, HEAD_DIM), t_new)
        ob_s = ob_r[:, :t_new * N_HEADS_B].reshape(1, db * t_new, N_HEADS_B * HEAD_DIM).astype(BF16)

        x1 = _attn_out(ys, oa_s, ob_s, g_m, wo_b, ln1_g[l], ln1_b[l], alpha)
        ys = _ffn(x1, sh_f, sc_f, g_f, wu_b, wd_b, ln2_g[l], ln2_b[l], alpha)
        rows_s.append((ka.reshape(db, t_new, N_KV_A, HEAD_DIM), va.reshape(db, t_new, N_KV_A, HEAD_DIM),
                       ki.reshape(db, t_new, IDX_DIM),
                       kb.reshape(db, t_new, N_HEADS_B, HEAD_DIM), vb.reshape(db, t_new, N_HEADS_B, HEAD_DIM)))

    outs_p = [jnp.stack([r[i] for r in rows_p]) for i in range(5)]
    outs_s = [jnp.stack([r[i] for r in rows_s]) for i in range(5)]
    return (yp, ys.reshape(db, t_new, d), *outs_p, *outs_s)
```

```python
import functools

import numpy as np
import jax
import jax.numpy as jnp
from jax import lax
from jax.experimental import pallas as pl
from jax.experimental.pallas import tpu as pltpu

F32 = jnp.float32
BF16 = jnp.bfloat16
I32 = jnp.int32

HEAD_DIM = 64
N_HEADS_A = 8
N_KV_A = 2
N_HEADS_B = 8
N_IDX_HEADS = 8
IDX_DIM = 64
TOPK_MAX = 256
PAGE_SIZE = 128
ROPE_THETA = 10000.0
LN_EPS = 1e-5
LANES = 128
HALF = HEAD_DIM // 2
QK_SCALE = HEAD_DIM ** -0.5
QK_SCALE_LOG2 = QK_SCALE * 1.4426950408889634
IDX_SCALE = (IDX_DIM ** -0.5) * (N_IDX_HEADS ** -0.5)
INT_MIN = -(2 ** 31)
NEG_BIG = -1e30
SB_EXIT = -110.0
COUNT_STRIP = 128
VMEM_LIMIT = 56 * 1024 * 1024

_W_QA = N_HEADS_A * LANES
_W_KA = N_KV_A * HEAD_DIM
_W_QI = N_IDX_HEADS * IDX_DIM
_W_B = N_HEADS_B * HEAD_DIM
_OFF_QA = 0
_OFF_KA = _OFF_QA + _W_QA
_OFF_VA = _OFF_KA + _W_KA
_OFF_QI = _OFF_VA + _W_KA
_OFF_KW = _OFF_QI + _W_QI
_OFF_QB = _OFF_KW + LANES
_OFF_KB = _OFF_QB + _W_B
_OFF_VB = _OFF_KB + _W_B
_W_TOT = _OFF_VB + _W_B


def _nt(a, b):
    return lax.dot_general(a, b, (((1,), (1,)), ((), ())), preferred_element_type=F32)


def _params(sem):
    return pltpu.CompilerParams(dimension_semantics=sem, vmem_limit_bytes=VMEM_LIMIT)


def _mod_kernel(c_ref, w_ref, b_ref, o_ref):
    o_ref[...] = jnp.dot(c_ref[...], w_ref[...], preferred_element_type=F32,
                         precision=lax.Precision.HIGHEST) + b_ref[...]


def _modulation(c, w_mod, b_mod):
    r, d = c.shape
    n = w_mod.shape[1]
    tn = 1536 if n % 1536 == 0 else n
    return pl.pallas_call(
        _mod_kernel,
        grid=(n // tn,),
        in_specs=[pl.BlockSpec((r, d), lambda j: (0, 0)),
                  pl.BlockSpec((d, tn), lambda j: (0, j)),
                  pl.BlockSpec((1, tn), lambda j: (0, j))],
        out_specs=pl.BlockSpec((r, tn), lambda j: (0, j)),
        out_shape=jax.ShapeDtypeStruct((r, n), F32),
        compiler_params=_params(("arbitrary",)),
        name="modulation",
    )(c, w_mod, b_mod.reshape(1, n))


def _rope(x, cos, sin_signed):
    w = x.shape[1]
    lane = lax.broadcasted_iota(I32, x.shape, 1)
    partner = jnp.where(lane % HEAD_DIM < HALF, pltpu.roll(x, w - HALF, 1), pltpu.roll(x, HALF, 1))
    reps = w // LANES
    if reps > 1:
        cos = jnp.concatenate([cos] * reps, axis=1)
        sin_signed = jnp.concatenate([sin_signed] * reps, axis=1)
    return x * cos + partner * sin_signed


def _proj_kernel(x_ref, shift_ref, scale_ref, cos_ref, sin_ref, w_ref,
                 qa_ref, ka_ref, va_ref, qi_ref, ki_ref, wi_ref, qb_ref, kb_ref, vb_ref,
                 kab_ref, vab_ref, ki2_ref, kbb_ref, vbb_ref):
    h = (x_ref[0] * (1.0 + scale_ref[0]) + shift_ref[0]).astype(BF16)
    cos = cos_ref[...]
    sin = sin_ref[...]

    def mm(off, width):
        return jnp.dot(h, w_ref[:, off:off + width], preferred_element_type=F32)

    for hh in range(N_HEADS_A):
        q = _rope(mm(_OFF_QA + hh * LANES, LANES), cos, sin) * QK_SCALE_LOG2
        qa_ref[0, hh] = q.astype(BF16)
    ka = _rope(mm(_OFF_KA, _W_KA), cos, sin)
    ka_ref[0] = ka
    kab_ref[0] = ka.astype(BF16)
    va = mm(_OFF_VA, _W_KA)
    va_ref[0] = va
    vab_ref[0] = va.astype(BF16)
    qi = _rope(mm(_OFF_QI, _W_QI), cos, sin).astype(BF16)
    for j in range(N_IDX_HEADS // 2):
        qi_ref[0, j] = qi[:, j * LANES:(j + 1) * LANES]
    kw = mm(_OFF_KW, LANES)
    lane = lax.broadcasted_iota(I32, kw.shape, 1)
    ki = jnp.where(lane < IDX_DIM, _rope(kw, cos, sin), 0.0)
    ki_ref[0] = ki[:, :IDX_DIM]
    wi_ref[0] = kw
    kib = ki.astype(BF16)
    ki2_ref[0, :, :LANES] = kib
    ki2_ref[0, :, LANES:] = pltpu.roll(ki, IDX_DIM, 1).astype(BF16)
    qb_ref[0] = (mm(_OFF_QB, _W_B) * QK_SCALE).astype(BF16)
    kb = mm(_OFF_KB, _W_B)
    kb_ref[0] = kb
    kbb_ref[0] = kb.astype(BF16)
    vb = mm(_OFF_VB, _W_B)
    vb_ref[0] = vb
    vbb_ref[0] = vb.astype(BF16)


def _project(x, shift, scale, cos, sin, w_pad):
    g, r, d = x.shape
    tm = min(512, r)
    rm = shift.shape[1]
    mod_blk = (1, 1, d) if rm == 1 else (1, tm, d)
    mod_map = (lambda b, i: (b, 0, 0)) if rm == 1 else (lambda b, i: (b, i, 0))

    def out(width, dtype):
        return (pl.BlockSpec((1, tm, width), lambda b, i: (b, i, 0)),
                jax.ShapeDtypeStruct((g, r, width), dtype))

    def out_heads(n):
        return (pl.BlockSpec((1, n, tm, LANES), lambda b, i: (b, 0, i, 0)),
                jax.ShapeDtypeStruct((g, n, r, LANES), BF16))

    outs = [out_heads(N_HEADS_A), out(_W_KA, F32), out(_W_KA, F32), out_heads(N_IDX_HEADS // 2),
            out(IDX_DIM, F32), out(LANES, F32), out(_W_B, BF16), out(_W_B, F32), out(_W_B, F32),
            out(_W_KA, BF16), out(_W_KA, BF16), out(2 * LANES, BF16), out(_W_B, BF16), out(_W_B, BF16)]
    return pl.pallas_call(
        _proj_kernel,
        grid=(g, r // tm),
        in_specs=[pl.BlockSpec((1, tm, d), lambda b, i: (b, i, 0)),
                  pl.BlockSpec(mod_blk, mod_map),
                  pl.BlockSpec(mod_blk, mod_map),
                  pl.BlockSpec((tm, LANES), lambda b, i: (i, 0)),
                  pl.BlockSpec((tm, LANES), lambda b, i: (i, 0)),
                  pl.BlockSpec((d, _W_TOT), lambda b, i: (0, 0))],
        out_specs=[o[0] for o in outs],
        out_shape=[o[1] for o in outs],
        compiler_params=_params(("arbitrary", "arbitrary")),
        name="project",
    )(x, shift, scale, cos, sin, w_pad)


def _sortable(score):
    score = jnp.where(score == 0.0, 0.0, score)
    bits = pltpu.bitcast(score, I32)
    return jnp.where(bits < 0, bits ^ 0x7FFFFFFF, bits)


def _count_rows(keys_ref, rs, nblk, tk, pred):
    strip = rs.stop - rs.start

    def body(kb, acc):
        col0 = pl.multiple_of(kb * tk, tk)
        for c in range(tk // LANES):
            chunk = keys_ref[rs, pl.ds(col0 + c * LANES, LANES)]
            acc = acc + jnp.where(pred(chunk, col0 + c * LANES), 1, 0)
        return acc

    acc = lax.fori_loop(0, nblk, body, jnp.zeros((strip, LANES), I32))
    return jnp.broadcast_to(jnp.sum(acc, axis=1, keepdims=True), (strip, LANES))


def _topk_threshold(keys_ref, thr_ref, jmax_ref, nblk, tk, k, row_ok=None):
    rows = keys_ref.shape[0]
    col_bits = int(keys_ref.shape[1]).bit_length()
    strip = min(rows, COUNT_STRIP)
    lane = lax.broadcasted_iota(I32, (strip, LANES), 1)

    for r0 in range(0, rows, strip):
        rs = slice(r0, r0 + strip)

        def bit_body(i, ans, rs=rs):
            cand = ans + jnp.left_shift(jnp.int32(1), 31 - i)
            cnt = _count_rows(keys_ref, rs, nblk, tk, lambda chunk, col0: chunk >= cand)
            return jnp.where(cnt >= k, cand, ans)

        thr = lax.fori_loop(0, 32, bit_body, jnp.full((strip, LANES), INT_MIN, I32))
        n_gt = _count_rows(keys_ref, rs, nblk, tk, lambda chunk, col0: chunk > thr)
        n_eq = _count_rows(keys_ref, rs, nblk, tk, lambda chunk, col0: chunk == thr)
        need = k - n_gt
        live = thr != INT_MIN
        partial = live & (n_eq != need)
        if row_ok is not None:
            partial = partial & row_ok

        def tie_search(rs=rs, thr=thr, need=need):
            def jbody(i, j):
                cand = j + jnp.left_shift(jnp.int32(1), col_bits - 1 - i)
                cnt = _count_rows(keys_ref, rs, nblk, tk,
                                  lambda chunk, col0: (chunk == thr) & (lane < cand - col0))
                return jnp.where(cnt < need, cand, j)

            return lax.fori_loop(0, col_bits, jbody, jnp.zeros((strip, LANES), I32))

        any_partial = jnp.max(jnp.where(partial, 1.0, 0.0)) > 0.5
        jsearch = lax.cond(any_partial, tie_search, lambda: jnp.zeros((strip, LANES), I32))
        thr_ref[rs] = thr
        jmax_ref[rs] = jnp.where(live, jnp.where(partial, jsearch, jnp.int32(2 ** 30)), -1)


def _select_bias(keys_ref, rows, col0, width, thr, jmax):
    lane = lax.broadcasted_iota(I32, thr.shape, 1)
    out = []
    for c in range(width // LANES):
        key = keys_ref[rows, pl.ds(col0 + c * LANES, LANES)]
        sel = (key > thr) | ((key == thr) & (lane <= jmax - (col0 + c * LANES)))
        out.append(jnp.where(sel, 0.0, NEG_BIG))
    return out[0] if len(out) == 1 else jnp.concatenate(out, axis=1)


def _head_out_pairs(o_heads):
    pieces = []
    lane = lax.broadcasted_iota(I32, o_heads[0].shape, 1)
    group = N_HEADS_A // N_KV_A
    for j in range(N_HEADS_A // 2):
        a, b = o_heads[2 * j], o_heads[2 * j + 1]
        if (2 * j) // group == 0:
            pieces.append(jnp.where(lane < HEAD_DIM, a, pltpu.roll(b, HEAD_DIM, 1)))
        else:
            pieces.append(jnp.where(lane < HEAD_DIM, pltpu.roll(a, HEAD_DIM, 1), b))
    return jnp.concatenate(pieces, axis=1)


def _prompt_a_kernel(qi_ref, wi_ref, qa_ref, ki2_ref, ka_ref, va_ref, o_ref,
                     keys_ref, thr_ref, jmax_ref, m_ref, l_ref, acc_ref, *, tq, tk, ta, topk):
    qblk = pl.program_id(1)
    row = qblk * tq + lax.broadcasted_iota(I32, (tq, tk), 0)
    nblk = (qblk * tq + tq + tk - 1) // tk
    wv = wi_ref[0] * IDX_SCALE

    def score_body(kb, carry):
        col0 = pl.multiple_of(kb * tk, tk)
        kk = ki2_ref[0, pl.ds(col0, tk), :]
        k_lo, k_hi = kk[:, :LANES], kk[:, LANES:]
        score = jnp.zeros((tq, tk), F32)
        for j in range(N_IDX_HEADS // 2):
            qp = qi_ref[0, j]
            w0 = wv[:, IDX_DIM + 2 * j:IDX_DIM + 2 * j + 1]
            w1 = wv[:, IDX_DIM + 2 * j + 1:IDX_DIM + 2 * j + 2]
            score = score + w0 * jnp.maximum(_nt(qp, k_lo), 0.0)
            score = score + w1 * jnp.maximum(_nt(qp, k_hi), 0.0)
        col = col0 + lax.broadcasted_iota(I32, (tq, tk), 1)
        keys_ref[:, pl.ds(col0, tk)] = jnp.where(col <= row, _sortable(score), INT_MIN)
        return carry

    lax.fori_loop(0, nblk, score_body, 0)
    nblk_a = (qblk * tq + tq + ta - 1) // ta

    def pad_body(kb, carry):
        keys_ref[:, pl.ds(pl.multiple_of(kb * tk, tk), tk)] = jnp.full((tq, tk), INT_MIN, I32)
        return carry

    lax.fori_loop(nblk, nblk_a * (ta // tk), pad_body, 0)
    _topk_threshold(keys_ref, thr_ref, jmax_ref, nblk, tk, topk)
    thr, jmax = thr_ref[...], jmax_ref[...]

    m_ref[...] = jnp.full(m_ref.shape, NEG_BIG, F32)
    l_ref[...] = jnp.zeros(l_ref.shape, F32)
    acc_ref[...] = jnp.zeros(acc_ref.shape, F32)

    def attn_body(kb, carry):
        col0 = pl.multiple_of(kb * ta, ta)
        bias = _select_bias(keys_ref, slice(None), col0, ta, thr, jmax)
        kblk = ka_ref[0, pl.ds(col0, ta), :]
        vblk = va_ref[0, pl.ds(col0, ta), :]
        for hh in range(N_HEADS_A):
            s = _nt(qa_ref[0, hh], kblk) + bias
            m_old = m_ref[hh]
            m_new = jnp.maximum(m_old, jnp.max(s, axis=1, keepdims=True))
            m_safe = jnp.where(m_new < 0.5 * NEG_BIG, 0.0, m_new)
            p = jnp.exp2(s - m_safe[:, :1])
            alpha = jnp.exp2(m_old - m_new)
            l_ref[hh] = alpha * l_ref[hh] + jnp.sum(p, axis=1, keepdims=True)
            acc_ref[hh] = alpha * acc_ref[hh] + jnp.dot(p.astype(BF16), vblk, preferred_element_type=F32)
            m_ref[hh] = m_new
        return carry

    lax.fori_loop(0, nblk_a, attn_body, 0)
    o_ref[0] = _head_out_pairs([acc_ref[hh] / l_ref[hh] for hh in range(N_HEADS_A)]).astype(o_ref.dtype)


def _prompt_group_a(qi, wi, qa, ki2, kab, vab):
    b, npair, s, _ = qi.shape
    tq = min(256, s)
    tk = min(512, s)
    ta = min(1024, s)
    topk = min(TOPK_MAX, s // 4)
    kern = functools.partial(_prompt_a_kernel, tq=tq, tk=tk, ta=ta, topk=topk)
    return pl.pallas_call(
        kern,
        grid=(b, s // tq),
        in_specs=[pl.BlockSpec((1, npair, tq, LANES), lambda bb, i: (bb, 0, i, 0)),
                  pl.BlockSpec((1, tq, LANES), lambda bb, i: (bb, i, 0)),
                  pl.BlockSpec((1, N_HEADS_A, tq, LANES), lambda bb, i: (bb, 0, i, 0)),
                  pl.BlockSpec((1, s, 2 * LANES), lambda bb, i: (bb, 0, 0)),
                  pl.BlockSpec((1, s, LANES), lambda bb, i: (bb, 0, 0)),
                  pl.BlockSpec((1, s, LANES), lambda bb, i: (bb, 0, 0))],
        out_specs=pl.BlockSpec((1, tq, N_HEADS_A * HEAD_DIM), lambda bb, i: (bb, i, 0)),
        out_shape=jax.ShapeDtypeStruct((b, s, N_HEADS_A * HEAD_DIM), BF16),
        scratch_shapes=[pltpu.VMEM((tq, s), I32),
                        pltpu.VMEM((tq, LANES), I32),
                        pltpu.VMEM((tq, LANES), I32),
                        pltpu.VMEM((N_HEADS_A, tq, LANES), F32),
                        pltpu.VMEM((N_HEADS_A, tq, LANES), F32),
                        pltpu.VMEM((N_HEADS_A, tq, LANES), F32)],
        compiler_params=_params(("arbitrary", "arbitrary")),
        name="prompt_group_a",
    )(qi, wi, qa, ki2, kab, vab)


def _log_sigmoid(z):
    return jnp.minimum(z, 0.0) - jnp.log(1.0 + jnp.exp(-jnp.abs(z)))


def _split3(x):
    hi = x.astype(BF16)
    r1 = x - hi.astype(F32)
    mid = r1.astype(BF16)
    lo = (r1 - mid.astype(F32)).astype(BF16)
    return hi, mid, lo


def _prompt_b_kernel(q_ref, k_ref, v_ref, o_ref, acc_ref, *, tq, tk):
    qblk = pl.program_id(2)
    q = q_ref[0]
    lane_q = lax.broadcasted_iota(I32, q.shape, 1)
    row = qblk * tq + lax.broadcasted_iota(I32, (tq, tk), 0)
    jj = lax.broadcasted_iota(I32, (tk, tk), 0)
    ss = lax.broadcasted_iota(I32, (tk, tk), 1)
    tri = jnp.where(jj > ss, 1.0, 0.0).astype(BF16)
    kb0 = (qblk * tq + tq - 1) // tk

    for sub in range(2):
        qm = jnp.where((lane_q < HEAD_DIM) == (sub == 0), q, jnp.zeros_like(q))
        acc_ref[sub] = jnp.zeros((tq, LANES), F32)

        def cond(carry):
            kb, _, cmax = carry
            return (kb >= 0) & (cmax > SB_EXIT)

        def body(carry):
            kb, c, _ = carry
            col0 = pl.multiple_of(kb * tk, tk)
            z = _nt(qm, k_ref[0, pl.ds(col0, tk), :])
            valid = (col0 + lax.broadcasted_iota(I32, (tq, tk), 1)) < row
            ls = _log_sigmoid(z)
            lk = jnp.where(valid, ls - z, 0.0)
            hi, mid, lo = _split3(lk)
            after = (jnp.dot(hi, tri, preferred_element_type=F32)
                     + jnp.dot(mid, tri, preferred_element_type=F32)
                     + jnp.dot(lo, tri, preferred_element_type=F32))
            a = jnp.where(valid, jnp.exp(ls + after + c), 0.0)
            acc_ref[sub] += jnp.dot(a.astype(BF16), v_ref[0, pl.ds(col0, tk), :],
                                    preferred_element_type=F32)
            c = c + jnp.sum(lk, axis=1, keepdims=True)
            return kb - 1, c, jnp.max(c)

        lax.while_loop(cond, body, (kb0, jnp.zeros((tq, 1), F32), jnp.float32(0.0)))

    lane = lax.broadcasted_iota(I32, (tq, LANES), 1)
    o_ref[0] = jnp.where(lane < HEAD_DIM, acc_ref[0], acc_ref[1]).astype(o_ref.dtype)


def _prompt_group_b(qb, kbb, vbb):
    b, s, _ = qb.shape
    tq = min(256, s)
    tk = min(256, s)
    npair = N_HEADS_B // 2
    kern = functools.partial(_prompt_b_kernel, tq=tq, tk=tk)
    return pl.pallas_call(
        kern,
        grid=(b, npair, s // tq),
        in_specs=[pl.BlockSpec((1, tq, LANES), lambda bb, j, i: (bb, i, j)),
                  pl.BlockSpec((1, s, LANES), lambda bb, j, i: (bb, 0, j)),
                  pl.BlockSpec((1, s, LANES), lambda bb, j, i: (bb, 0, j))],
        out_specs=pl.BlockSpec((1, tq, LANES), lambda bb, j, i: (bb, i, j)),
        out_shape=jax.ShapeDtypeStruct((b, s, N_HEADS_B * HEAD_DIM), BF16),
        scratch_shapes=[pltpu.VMEM((2, tq, LANES), F32)],
        compiler_params=_params(("arbitrary", "arbitrary", "arbitrary")),
        name="prompt_group_b",
    )(qb, kbb, vbb)


def _sample_a_kernel(pt_ref, qi_ref, w_ref, qa_ref, kin_ref, kan_ref, van_ref,
                     cki_ref, cka_ref, cva_ref, o_ref,
                     ki_buf, ka_buf, va_buf, sems, keys_ref, thr_ref, jmax_ref, s_ref,
                     *, n_pages, n_new, topk, chunk):
    b = pl.program_id(0)
    past = n_pages * PAGE_SIZE
    tpad = keys_ref.shape[0]
    rows = qa_ref.shape[1]
    slot = b % 2

    def page_copies(seq, buf_slot, p):
        phys = pt_ref[seq, p]
        dst = pl.ds(pl.multiple_of(p * PAGE_SIZE, PAGE_SIZE), PAGE_SIZE)
        return (pltpu.make_async_copy(cki_ref.at[phys], ki_buf.at[buf_slot, :, dst], sems.at[buf_slot, 0]),
                pltpu.make_async_copy(cka_ref.at[phys], ka_buf.at[buf_slot, :, dst], sems.at[buf_slot, 1]),
                pltpu.make_async_copy(cva_ref.at[phys], va_buf.at[buf_slot, :, dst], sems.at[buf_slot, 2]))

    def start_seq(seq, buf_slot):
        def body(p, carry):
            for cp in page_copies(seq, buf_slot, p):
                cp.start()
            return carry

        lax.fori_loop(0, n_pages, body, 0)

    def wait_seq(seq, buf_slot):
        def body(p, carry):
            for cp in page_copies(seq, buf_slot, p):
                cp.wait()
            return carry

        lax.fori_loop(0, n_pages, body, 0)

    @pl.when(b == 0)
    def _():
        start_seq(0, 0)

    @pl.when(b + 1 < pl.num_programs(0))
    def _():
        start_seq(b + 1, 1 - slot)

    wait_seq(b, slot)

    qi = qi_ref[0]
    w = w_ref[0] * IDX_SCALE
    tok = lax.broadcasted_iota(I32, (tpad, LANES), 0)

    def scores_of(kt):
        d = jnp.maximum(jnp.dot(qi, kt, preferred_element_type=F32), 0.0) * w
        return jnp.sum(d.reshape(tpad, N_IDX_HEADS, kt.shape[1]), axis=1)

    def score_body(c, carry):
        col0 = pl.multiple_of(c * chunk, chunk)
        sc = scores_of(ki_buf[slot, :, pl.ds(col0, chunk)].astype(BF16))
        keys_ref[:, pl.ds(col0, chunk)] = _sortable(sc)
        return carry

    lax.fori_loop(0, past // chunk, score_body, 0)
    new_col = lax.broadcasted_iota(I32, (tpad, LANES), 1)
    new_ok = (new_col <= tok) & (new_col < n_new)
    keys_ref[:, past:past + LANES] = jnp.where(new_ok, _sortable(scores_of(kin_ref[0])), INT_MIN)

    width = past + LANES
    sel_tk = max(t for t in range(LANES, 2048 + 1, LANES) if width % t == 0)
    _topk_threshold(keys_ref, thr_ref, jmax_ref, width // sel_tk, sel_tk, topk, row_ok=tok < n_new)
    thr, jmax = thr_ref[...], jmax_ref[...]

    qa = qa_ref[0]

    def masked_scores(kt, col0):
        n = kt.shape[1]
        bias = _select_bias(keys_ref, slice(None), col0, n, thr, jmax)
        s = jnp.dot(qa, kt, preferred_element_type=F32).reshape(tpad, N_HEADS_A, n) + bias[:, None, :]
        return s.reshape(rows, n)

    def s_body(c, m):
        col0 = pl.multiple_of(c * chunk, chunk)
        s = masked_scores(ka_buf[slot, :, pl.ds(col0, chunk)].astype(BF16), col0)
        s_ref[:, pl.ds(col0, chunk)] = s
        return jnp.maximum(m, jnp.max(s, axis=1, keepdims=True))

    m = lax.fori_loop(0, past // chunk, s_body, jnp.full((rows, 1), NEG_BIG, F32))
    s_new = masked_scores(kan_ref[0], past)
    m = jnp.maximum(m, jnp.max(s_new, axis=1, keepdims=True))

    p_new = jnp.exp2(s_new - m)
    l0 = jnp.sum(p_new, axis=1, keepdims=True)
    acc0 = _nt(p_new.astype(BF16), van_ref[0])

    def pv_body(c, carry):
        l, acc = carry
        col0 = pl.multiple_of(c * chunk, chunk)
        p = jnp.exp2(s_ref[:, pl.ds(col0, chunk)] - m)
        acc = acc + _nt(p.astype(BF16), va_buf[slot, :, pl.ds(col0, chunk)].astype(BF16))
        return l + jnp.sum(p, axis=1, keepdims=True), acc

    l, acc = lax.fori_loop(0, past // chunk, pv_body, (l0, acc0))
    o_ref[0] = acc / l


def _sample_group_a(page_table, qi_r, w_r, qa_r, kin, kan, van, cki, cka, cva, n_new):
    db, n_pages = page_table.shape
    past = n_pages * PAGE_SIZE
    rows = qa_r.shape[1]
    tpad = rows // N_HEADS_A
    topk = min(TOPK_MAX, (past + n_new) // 4)
    chunk = min(1024, past)
    kern = functools.partial(_sample_a_kernel, n_pages=n_pages, n_new=n_new, topk=topk, chunk=chunk)
    grid_spec = pltpu.PrefetchScalarGridSpec(
        num_scalar_prefetch=1,
        grid=(db,),
        in_specs=[pl.BlockSpec((1, rows, IDX_DIM), lambda i, pt: (i, 0, 0)),
                  pl.BlockSpec((1, rows, 1), lambda i, pt: (i, 0, 0)),
                  pl.BlockSpec((1, rows, LANES), lambda i, pt: (i, 0, 0)),
                  pl.BlockSpec((1, IDX_DIM, LANES), lambda i, pt: (i, 0, 0)),
                  pl.BlockSpec((1, LANES, LANES), lambda i, pt: (i, 0, 0)),
                  pl.BlockSpec((1, LANES, LANES), lambda i, pt: (i, 0, 0)),
                  pl.BlockSpec(memory_space=pl.ANY),
                  pl.BlockSpec(memory_space=pl.ANY),
                  pl.BlockSpec(memory_space=pl.ANY)],
        out_specs=pl.BlockSpec((1, rows, LANES), lambda i, pt: (i, 0, 0)),
        scratch_shapes=[pltpu.VMEM((2, IDX_DIM, past), F32),
                        pltpu.VMEM((2, LANES, past), F32),
                        pltpu.VMEM((2, LANES, past), F32),
                        pltpu.SemaphoreType.DMA((2, 3)),
                        pltpu.VMEM((tpad, past + LANES), I32),
                        pltpu.VMEM((tpad, LANES), I32),
                        pltpu.VMEM((tpad, LANES), I32),
                        pltpu.VMEM((rows, past), F32)])
    return pl.pallas_call(
        kern,
        grid_spec=grid_spec,
        out_shape=jax.ShapeDtypeStruct((db, rows, LANES), F32),
        compiler_params=_params(("arbitrary",)),
        name="sample_group_a",
    )(page_table, qi_r, w_r, qa_r, kin, kan, van, cki, cka, cva)


def _sample_b_kernel(pt_ref, q_ref, kn_ref, vn_ref, ck_ref, cv_ref, o_ref,
                     k_buf, v_buf, sems, *, n_pages):
    b = pl.program_id(0)
    q = q_ref[0]
    rows = q.shape[0]
    jj = lax.broadcasted_iota(I32, (PAGE_SIZE, PAGE_SIZE), 0)
    ss = lax.broadcasted_iota(I32, (PAGE_SIZE, PAGE_SIZE), 1)
    tri = jnp.where(jj > ss, 1.0, 0.0).astype(BF16)
    last = n_pages - 1

    def copies(p, slot):
        phys = pt_ref[b, p]
        return (pltpu.make_async_copy(ck_ref.at[phys], k_buf.at[slot], sems.at[slot, 0]),
                pltpu.make_async_copy(cv_ref.at[phys], v_buf.at[slot], sems.at[slot, 1]))

    def block(kt, vt, valid, c):
        z = jnp.dot(q, kt, preferred_element_type=F32)
        ls = _log_sigmoid(z)
        lk = ls - z
        if valid is not None:
            lk = jnp.where(valid, lk, 0.0)
        hi, mid, lo = _split3(lk)
        after = (jnp.dot(hi, tri, preferred_element_type=F32)
                 + jnp.dot(mid, tri, preferred_element_type=F32)
                 + jnp.dot(lo, tri, preferred_element_type=F32))
        a = jnp.exp(ls + after + c)
        if valid is not None:
            a = jnp.where(valid, a, 0.0)
        return c + jnp.sum(lk, axis=1, keepdims=True), _nt(a.astype(BF16), vt)

    for cp in copies(last, 0):
        cp.start()

    tok = lax.broadcasted_iota(I32, (rows, PAGE_SIZE), 0) // N_HEADS_B
    valid_new = lax.broadcasted_iota(I32, (rows, PAGE_SIZE), 1) < tok
    c, o = block(kn_ref[0], vn_ref[0], valid_new, jnp.zeros((rows, 1), F32))

    def cond(carry):
        p, _, _, cmax = carry
        return (p >= 0) & (cmax > SB_EXIT)

    def body(carry):
        p, c, o, _ = carry
        slot = (last - p) % 2
        for cp in copies(p, slot):
            cp.wait()

        @pl.when(p > 0)
        def _():
            for cp in copies(p - 1, 1 - slot):
                cp.start()

        c, o_add = block(k_buf[slot].astype(BF16), v_buf[slot].astype(BF16), None, c)
        return p - 1, c, o + o_add, jnp.max(c)

    p_end, _, o, _ = lax.while_loop(cond, body, (jnp.int32(last), c, o, jnp.max(c)))

    @pl.when(p_end >= 0)
    def _():
        for cp in copies(p_end, (last - p_end) % 2):
            cp.wait()

    o_ref[0] = o


def _sample_group_b(page_table, q_bd, knt, vnt, ckb, cvb):
    db, n_pages = page_table.shape
    rows, width = q_bd.shape[1:]
    kern = functools.partial(_sample_b_kernel, n_pages=n_pages)
    grid_spec = pltpu.PrefetchScalarGridSpec(
        num_scalar_prefetch=1,
        grid=(db,),
        in_specs=[pl.BlockSpec((1, rows, width), lambda i, pt: (i, 0, 0)),
                  pl.BlockSpec((1, width, PAGE_SIZE), lambda i, pt: (i, 0, 0)),
                  pl.BlockSpec((1, width, PAGE_SIZE), lambda i, pt: (i, 0, 0)),
                  pl.BlockSpec(memory_space=pl.ANY),
                  pl.BlockSpec(memory_space=pl.ANY)],
        out_specs=pl.BlockSpec((1, rows, width), lambda i, pt: (i, 0, 0)),
        scratch_shapes=[pltpu.VMEM((2, width, PAGE_SIZE), F32),
                        pltpu.VMEM((2, width, PAGE_SIZE), F32),
                        pltpu.SemaphoreType.DMA((2, 2))])
    return pl.pallas_call(
        kern,
        grid_spec=grid_spec,
        out_shape=jax.ShapeDtypeStruct((db, rows, width), F32),
        compiler_params=_params(("arbitrary",)),
        name="sample_group_b",
    )(page_table, q_bd, knt, vnt, ckb, cvb)


def _layernorm(x, g, b):
    mu = jnp.mean(x, axis=-1, keepdims=True)
    xc = x - mu
    var = jnp.mean(xc * xc, axis=-1, keepdims=True)
    return xc * lax.rsqrt(var + LN_EPS) * g + b


def _attn_out_kernel(x_ref, oa_ref, ob_ref, gate_ref, wo_ref, g_ref, b_ref, o_ref, *, alpha):
    half = oa_ref.shape[2]
    mix = (jnp.dot(oa_ref[0], wo_ref[:half, :], preferred_element_type=F32)
           + jnp.dot(ob_ref[0], wo_ref[half:, :], preferred_element_type=F32))
    o_ref[0] = _layernorm(alpha * x_ref[0] + gate_ref[0] * mix, g_ref[...], b_ref[...])


def _attn_out(x, oa, ob, gate, w_o, g, bta, alpha):
    gg, r, d = x.shape
    tm = min(512, r)
    rm = gate.shape[1]
    mod_blk = (1, 1, d) if rm == 1 else (1, tm, d)
    mod_map = (lambda b, i: (b, 0, 0)) if rm == 1 else (lambda b, i: (b, i, 0))
    wa = oa.shape[2]
    return pl.pallas_call(
        functools.partial(_attn_out_kernel, alpha=alpha),
        grid=(gg, r // tm),
        in_specs=[pl.BlockSpec((1, tm, d), lambda b, i: (b, i, 0)),
                  pl.BlockSpec((1, tm, wa), lambda b, i: (b, i, 0)),
                  pl.BlockSpec((1, tm, wa), lambda b, i: (b, i, 0)),
                  pl.BlockSpec(mod_blk, mod_map),
                  pl.BlockSpec((2 * wa, d), lambda b, i: (0, 0)),
                  pl.BlockSpec((1, d), lambda b, i: (0, 0)),
                  pl.BlockSpec((1, d), lambda b, i: (0, 0))],
        out_specs=pl.BlockSpec((1, tm, d), lambda b, i: (b, i, 0)),
        out_shape=jax.ShapeDtypeStruct((gg, r, d), F32),
        compiler_params=_params(("arbitrary", "arbitrary")),
        name="attn_out_ln",
    )(x, oa, ob, gate, w_o, g.reshape(1, d), bta.reshape(1, d))


def _ffn_kernel(x_ref, shift_ref, scale_ref, gate_ref, wu_ref, wd_ref, g_ref, b_ref, o_ref,
                h_ref, acc_ref, *, alpha):
    f = pl.program_id(2)

    @pl.when(f == 0)
    def _():
        h_ref[...] = (x_ref[0] * (1.0 + scale_ref[0]) + shift_ref[0]).astype(BF16)
        acc_ref[...] = jnp.zeros(acc_ref.shape, F32)

    u = jnp.maximum(jnp.dot(h_ref[...], wu_ref[...], preferred_element_type=F32), 0.0)
    acc_ref[...] += jnp.dot((u * u).astype(BF16), wd_ref[...], preferred_element_type=F32)

    @pl.when(f == pl.num_programs(2) - 1)
    def _():
        o_ref[0] = _layernorm(alpha * x_ref[0] + gate_ref[0] * acc_ref[...], g_ref[...], b_ref[...])


def _ffn(x, shift, scale, gate, w_up, w_down, g, bta, alpha):
    gg, r, d = x.shape
    dff = w_up.shape[1]
    tm = min(512, r)
    tf = min(1024, dff)
    rm = gate.shape[1]
    mod_blk = (1, 1, d) if rm == 1 else (1, tm, d)
    mod_map = (lambda b, i, f: (b, 0, 0)) if rm == 1 else (lambda b, i, f: (b, i, 0))
    return pl.pallas_call(
        functools.partial(_ffn_kernel, alpha=alpha),
        grid=(gg, r // tm, dff // tf),
        in_specs=[pl.BlockSpec((1, tm, d), lambda b, i, f: (b, i, 0)),
                  pl.BlockSpec(mod_blk, mod_map),
                  pl.BlockSpec(mod_blk, mod_map),
                  pl.BlockSpec(mod_blk, mod_map),
                  pl.BlockSpec((d, tf), lambda b, i, f: (0, f)),
                  pl.BlockSpec((tf, d), lambda b, i, f: (f, 0)),
                  pl.BlockSpec((1, d), lambda b, i, f: (0, 0)),
                  pl.BlockSpec((1, d), lambda b, i, f: (0, 0))],
        out_specs=pl.BlockSpec((1, tm, d), lambda b, i, f: (b, i, 0)),
        out_shape=jax.ShapeDtypeStruct((gg, r, d), F32),
        scratch_shapes=[pltpu.VMEM((tm, d), BF16), pltpu.VMEM((tm, d), F32)],
        compiler_params=_params(("arbitrary", "arbitrary", "arbitrary")),
        name="ffn_ln",
    )(x, shift, scale, gate, w_up, w_down, g.reshape(1, d), bta.reshape(1, d))


def _rope_tables(pos):
    inv = 1.0 / (ROPE_THETA ** (jnp.arange(HALF, dtype=F32) * (2.0 / HEAD_DIM)))
    ang = pos.astype(F32)[:, None] * inv[None, :]
    cos = jnp.tile(jnp.cos(ang), (1, 2 * LANES // HEAD_DIM))
    sin = jnp.sin(ang)
    sin = jnp.tile(jnp.concatenate([-sin, sin], axis=1), (1, LANES // HEAD_DIM))
    return cos, sin


def _pad_w_in(w_in_l):
    d = w_in_l.shape[0]
    group = N_HEADS_A // N_KV_A
    o_qa, o_ka, o_va = 0, 512, 640
    o_qi, o_ki, o_wi, o_qb, o_kb, o_vb = 768, 1280, 1344, 1352, 1864, 2376
    cols = []
    zero = jnp.zeros((d, HEAD_DIM), w_in_l.dtype)
    for hh in range(N_HEADS_A):
        wq = w_in_l[:, o_qa + hh * HEAD_DIM:o_qa + (hh + 1) * HEAD_DIM]
        cols += [wq, zero] if hh // group == 0 else [zero, wq]
    cols.append(w_in_l[:, o_ka:o_qi])
    cols.append(w_in_l[:, o_qi:o_ki])
    cols.append(w_in_l[:, o_ki:o_qb])
    cols.append(jnp.zeros((d, LANES - IDX_DIM - N_IDX_HEADS), w_in_l.dtype))
    cols.append(w_in_l[:, o_qb:])
    w = jnp.concatenate(cols, axis=1)
    assert w.shape[1] == _W_TOT
    return w.astype(BF16)


def kernel(x_prompt, x_sample, cache_k_a, cache_v_a, cache_k_idx, cache_k_b, cache_v_b, page_table,
           c_prompt, c_sample, w_mod, b_mod, w_in, w_o, ln1_g, ln1_b, w_up, w_down, ln2_g, ln2_b):
    depth = w_in.shape[0]
    alpha = float((2.0 * depth) ** 0.25)
    bsz, seq, d = x_prompt.shape
    db, t_new, _ = x_sample.shape
    n_pages = page_table.shape[1]
    past = n_pages * PAGE_SIZE
    n_phys = cache_k_a.shape[1]
    tpad = 8

    def page_major(cache):
        nd = cache.ndim
        t = jnp.transpose(cache, (0, 1) + tuple(range(3, nd)) + (2,))
        return t.reshape(depth, n_phys, -1, PAGE_SIZE)

    cki_t, cka_t, cva_t = page_major(cache_k_idx), page_major(cache_k_a), page_major(cache_v_a)
    ckb_t, cvb_t = page_major(cache_k_b), page_major(cache_v_b)

    cos_p, sin_p = _rope_tables(jnp.arange(seq, dtype=I32))
    cos_s, sin_s = _rope_tables(past + jnp.arange(t_new, dtype=I32))
    cos_s = jnp.tile(cos_s, (db, 1))
    sin_s = jnp.tile(sin_s, (db, 1))

    yp = x_prompt
    ys = x_sample.reshape(1, db * t_new, d)
    rows_p, rows_s = [], []
    for l in range(depth):
        w_pad = _pad_w_in(w_in[l])
        wo_b, wu_b, wd_b = w_o[l].astype(BF16), w_up[l].astype(BF16), w_down[l].astype(BF16)

        mod_p = _modulation(c_prompt, w_mod[l], b_mod[l])[:, None, :]
        sh_m, sc_m, g_m, sh_f, sc_f, g_f = jnp.split(mod_p, 6, axis=-1)
        (qa, ka, va, qi, ki, wi, qb, kb, vb, kab, vab, ki2, kbb, vbb) = _project(
            yp, sh_m, sc_m, cos_p, sin_p, w_pad)
        oa = _prompt_group_a(qi, wi, qa, ki2, kab, vab)
        ob = _prompt_group_b(qb, kbb, vbb)
        x1 = _attn_out(yp, oa, ob, g_m, wo_b, ln1_g[l], ln1_b[l], alpha)
        yp = _ffn(x1, sh_f, sc_f, g_f, wu_b, wd_b, ln2_g[l], ln2_b[l], alpha)
        rows_p.append((ka.reshape(bsz, seq, N_KV_A, HEAD_DIM), va.reshape(bsz, seq, N_KV_A, HEAD_DIM), ki,
                       kb.reshape(bsz, seq, N_HEADS_B, HEAD_DIM), vb.reshape(bsz, seq, N_HEADS_B, HEAD_DIM)))

        mod_s = _modulation(c_sample, w_mod[l], b_mod[l])
        mod_s = jnp.broadcast_to(mod_s[:, None, :], (db, t_new, 6 * d)).reshape(1, db * t_new, 6 * d)
        sh_m, sc_m, g_m, sh_f, sc_f, g_f = jnp.split(mod_s, 6, axis=-1)
        (qa, ka, va, qi, ki, wi, qb, kb, vb, kab, vab, ki2, kbb, vbb) = _project(
            ys, sh_m, sc_m, cos_s, sin_s, w_pad)

        def pad_tokens(a, n):
            return jnp.pad(a, [(0, 0), (0, n - a.shape[1])] + [(0, 0)] * (a.ndim - 2))

        qi_r = qi[0].reshape(N_IDX_HEADS // 2, db, t_new, 2, IDX_DIM).transpose(1, 2, 0, 3, 4)
        qi_r = pad_tokens(qi_r.reshape(db, t_new, N_IDX_HEADS, IDX_DIM), tpad)
        qi_r = qi_r.reshape(db, tpad * N_IDX_HEADS, IDX_DIM)
        w_r = pad_tokens(wi.reshape(db, t_new, LANES)[:, :, IDX_DIM:IDX_DIM + N_IDX_HEADS], tpad)
        w_r = w_r.reshape(db, tpad * N_IDX_HEADS, 1)
        qa_r = qa[0].reshape(N_HEADS_A, db, t_new, LANES).transpose(1, 2, 0, 3)
        qa_r = pad_tokens(qa_r, tpad).reshape(db, tpad * N_HEADS_A, LANES)

        def new_page(a):
            a = jnp.swapaxes(a.reshape(db, t_new, a.shape[-1]), 1, 2)
            return jnp.pad(a, [(0, 0), (0, 0), (0, PAGE_SIZE - t_new)]).astype(BF16)

        oa_r = _sample_group_a(page_table, qi_r, w_r, qa_r, new_page(ki), new_page(kab), new_page(vab),
                               cki_t[l], cka_t[l], cva_t[l], t_new)
        oa_r = oa_r.reshape(db, tpad, N_KV_A, N_HEADS_A // N_KV_A, N_KV_A, HEAD_DIM)[:, :t_new]
        oa_s = jnp.stack([oa_r[:, :, n, :, n, :] for n in range(N_KV_A)], axis=2)
        oa_s = oa_s.reshape(1, db * t_new, N_HEADS_A * HEAD_DIM).astype(BF16)

        eye = jnp.eye(N_HEADS_B, dtype=BF16)
        q_bd = jnp.einsum('bthd,gh->btghd', qb.reshape(db, t_new, N_HEADS_B, HEAD_DIM), eye)
        q_bd = q_bd.reshape(db, t_new * N_HEADS_B, N_HEADS_B * HEAD_DIM)
        ob_r = _sample_group_b(page_table, q_bd, new_page(kbb), new_page(vbb), ckb_t[l], cvb_t[l])
        ob_r = ob_r.reshape(db, t_new, N_HEADS_B, N_HEADS_B, HEAD_DIM)
        ob_s = jnp.stack([ob_r[:, :, hh, hh, :] for hh in range(N_HEADS_B)], axis=2)
        ob_s = ob_s.reshape(1, db * t_new, N_HEADS_B * HEAD_DIM).astype(BF16)

        x1 = _attn_out(ys, oa_s, ob_s, g_m, wo_b, ln1_g[l], ln1_b[l], alpha)
        ys = _ffn(x1, sh_f, sc_f, g_f, wu_b, wd_b, ln2_g[l], ln2_b[l], alpha)
        rows_s.append((ka.reshape(db, t_new, N_KV_A, HEAD_DIM), va.reshape(db, t_new, N_KV_A, HEAD_DIM),
                       ki.reshape(db, t_new, IDX_DIM),
                       kb.reshape(db, t_new, N_HEADS_B, HEAD_DIM), vb.reshape(db, t_new, N_HEADS_B, HEAD_DIM)))

    outs_p = [jnp.stack([r[i] for r in rows_p]) for i in range(5)]
    outs_s = [jnp.stack([r[i] for r in rows_s]) for i in range(5)]
    return (yp, ys.reshape(db, t_new, d), *outs_p, *outs_s)
```

```python
import functools

import numpy as np
import jax
import jax.numpy as jnp
from jax import lax
from jax.experimental import pallas as pl
from jax.experimental.pallas import tpu as pltpu

F32 = jnp.float32
BF16 = jnp.bfloat16
I32 = jnp.int32
I16 = jnp.int16
HALF_RANGE = 2 ** 15

HEAD_DIM = 64
N_HEADS_A = 8
N_KV_A = 2
N_HEADS_B = 8
N_IDX_HEADS = 8
IDX_DIM = 64
TOPK_MAX = 256
PAGE_SIZE = 128
ROPE_THETA = 10000.0
LN_EPS = 1e-5
LANES = 128
HALF = HEAD_DIM // 2
QK_SCALE = HEAD_DIM ** -0.5
QK_SCALE_LOG2 = QK_SCALE * 1.4426950408889634
IDX_SCALE = (IDX_DIM ** -0.5) * (N_IDX_HEADS ** -0.5)
INT_MIN = -(2 ** 31)
NEG_BIG = -1e30
SB_EXIT = -110.0
COUNT_STRIP = 128
VMEM_LIMIT = 56 * 1024 * 1024

_W_QA = N_HEADS_A * LANES
_W_KA = N_KV_A * HEAD_DIM
_W_QI = N_IDX_HEADS * IDX_DIM
_W_B = N_HEADS_B * HEAD_DIM
_OFF_QA = 0
_OFF_KA = _OFF_QA + _W_QA
_OFF_VA = _OFF_KA + _W_KA
_OFF_QI = _OFF_VA + _W_KA
_OFF_KW = _OFF_QI + _W_QI
_OFF_QB = _OFF_KW + LANES
_OFF_KB = _OFF_QB + _W_B
_OFF_VB = _OFF_KB + _W_B
_W_TOT = _OFF_VB + _W_B


def _nt(a, b):
    return lax.dot_general(a, b, (((1,), (1,)), ((), ())), preferred_element_type=F32)


def _params(sem):
    return pltpu.CompilerParams(dimension_semantics=sem, vmem_limit_bytes=VMEM_LIMIT)


def _mod_kernel(c_ref, w_ref, b_ref, o_ref):
    o_ref[...] = jnp.dot(c_ref[...], w_ref[...], preferred_element_type=F32,
                         precision=lax.Precision.HIGHEST) + b_ref[...]


def _modulation(c, w_mod, b_mod):
    r, d = c.shape
    n = w_mod.shape[1]
    tn = 1536 if n % 1536 == 0 else n
    return pl.pallas_call(
        _mod_kernel,
        grid=(n // tn,),
        in_specs=[pl.BlockSpec((r, d), lambda j: (0, 0)),
                  pl.BlockSpec((d, tn), lambda j: (0, j)),
                  pl.BlockSpec((1, tn), lambda j: (0, j))],
        out_specs=pl.BlockSpec((r, tn), lambda j: (0, j)),
        out_shape=jax.ShapeDtypeStruct((r, n), F32),
        compiler_params=_params(("arbitrary",)),
        name="modulation",
    )(c, w_mod, b_mod.reshape(1, n))


def _rope(x, cos, sin_signed):
    w = x.shape[1]
    lane = lax.broadcasted_iota(I32, x.shape, 1)
    partner = jnp.where(lane % HEAD_DIM < HALF, pltpu.roll(x, w - HALF, 1), pltpu.roll(x, HALF, 1))
    reps = w // LANES
    if reps > 1:
        cos = jnp.concatenate([cos] * reps, axis=1)
        sin_signed = jnp.concatenate([sin_signed] * reps, axis=1)
    return x * cos + partner * sin_signed


def _proj_kernel(x_ref, shift_ref, scale_ref, cos_ref, sin_ref, w_ref,
                 qa_ref, ka_ref, va_ref, qi_ref, ki_ref, wi_ref, qb_ref, kb_ref, vb_ref,
                 kab_ref, vab_ref, ki2_ref, kbb_ref, vbb_ref):
    h = (x_ref[0] * (1.0 + scale_ref[0]) + shift_ref[0]).astype(BF16)
    cos = cos_ref[...]
    sin = sin_ref[...]

    def mm(off, width):
        return jnp.dot(h, w_ref[:, off:off + width], preferred_element_type=F32)

    for hh in range(N_HEADS_A):
        q = _rope(mm(_OFF_QA + hh * LANES, LANES), cos, sin) * QK_SCALE_LOG2
        qa_ref[0, hh] = q.astype(BF16)
    ka = _rope(mm(_OFF_KA, _W_KA), cos, sin)
    ka_ref[0] = ka
    kab_ref[0] = ka.astype(BF16)
    va = mm(_OFF_VA, _W_KA)
    va_ref[0] = va
    vab_ref[0] = va.astype(BF16)
    qi = _rope(mm(_OFF_QI, _W_QI), cos, sin).astype(BF16)
    for j in range(N_IDX_HEADS // 2):
        qi_ref[0, j] = qi[:, j * LANES:(j + 1) * LANES]
    kw = mm(_OFF_KW, LANES)
    lane = lax.broadcasted_iota(I32, kw.shape, 1)
    ki = jnp.where(lane < IDX_DIM, _rope(kw, cos, sin), 0.0)
    ki_ref[0] = ki[:, :IDX_DIM]
    wi_ref[0] = kw
    kib = ki.astype(BF16)
    ki2_ref[0, :, :LANES] = kib
    ki2_ref[0, :, LANES:] = pltpu.roll(ki, IDX_DIM, 1).astype(BF16)
    qb_ref[0] = (mm(_OFF_QB, _W_B) * QK_SCALE).astype(BF16)
    kb = mm(_OFF_KB, _W_B)
    kb_ref[0] = kb
    kbb_ref[0] = kb.astype(BF16)
    vb = mm(_OFF_VB, _W_B)
    vb_ref[0] = vb
    vbb_ref[0] = vb.astype(BF16)


def _project(x, shift, scale, cos, sin, w_pad):
    g, r, d = x.shape
    tm = min(512, r)
    rm = shift.shape[1]
    mod_blk = (1, 1, d) if rm == 1 else (1, tm, d)
    mod_map = (lambda b, i: (b, 0, 0)) if rm == 1 else (lambda b, i: (b, i, 0))

    def out(width, dtype):
        return (pl.BlockSpec((1, tm, width), lambda b, i: (b, i, 0)),
                jax.ShapeDtypeStruct((g, r, width), dtype))

    def out_heads(n):
        return (pl.BlockSpec((1, n, tm, LANES), lambda b, i: (b, 0, i, 0)),
                jax.ShapeDtypeStruct((g, n, r, LANES), BF16))

    outs = [out_heads(N_HEADS_A), out(_W_KA, F32), out(_W_KA, F32), out_heads(N_IDX_HEADS // 2),
            out(IDX_DIM, F32), out(LANES, F32), out(_W_B, BF16), out(_W_B, F32), out(_W_B, F32),
            out(_W_KA, BF16), out(_W_KA, BF16), out(2 * LANES, BF16), out(_W_B, BF16), out(_W_B, BF16)]
    return pl.pallas_call(
        _proj_kernel,
        grid=(g, r // tm),
        in_specs=[pl.BlockSpec((1, tm, d), lambda b, i: (b, i, 0)),
                  pl.BlockSpec(mod_blk, mod_map),
                  pl.BlockSpec(mod_blk, mod_map),
                  pl.BlockSpec((tm, LANES), lambda b, i: (i, 0)),
                  pl.BlockSpec((tm, LANES), lambda b, i: (i, 0)),
                  pl.BlockSpec((d, _W_TOT), lambda b, i: (0, 0))],
        out_specs=[o[0] for o in outs],
        out_shape=[o[1] for o in outs],
        compiler_params=_params(("arbitrary", "arbitrary")),
        name="project",
    )(x, shift, scale, cos, sin, w_pad)


def _sortable(score):
    score = jnp.where(score == 0.0, 0.0, score)
    bits = pltpu.bitcast(score, I32)
    return jnp.where(bits < 0, bits ^ 0x7FFFFFFF, bits)


def _count_rows(keys_ref, rs, nblk, tk, pred):
    strip = rs.stop - rs.start
    preds = pred if isinstance(pred, (list, tuple)) else [pred]

    def body(kb, accs):
        col0 = pl.multiple_of(kb * tk, tk)
        accs = list(accs)
        for c in range(tk // LANES):
            chunk = keys_ref[rs, pl.ds(col0 + c * LANES, LANES)]
            for n, p in enumerate(preds):
                accs[n] = accs[n] + jnp.where(p(chunk, col0 + c * LANES), 1, 0)
        return tuple(accs)

    accs = lax.fori_loop(0, nblk, body, tuple(jnp.zeros((strip, LANES), I32) for _ in preds))
    cnts = [jnp.broadcast_to(jnp.sum(a, axis=1, keepdims=True), (strip, LANES)) for a in accs]
    return cnts if isinstance(pred, (list, tuple)) else cnts[0]


def _store_keys(keys_ref, hi_ref, lo_ref, rows, cols, key):
    keys_ref[rows, cols] = key
    hi_ref[rows, cols] = (key >> 16).astype(I16)
    lo_ref[rows, cols] = ((key & 0xFFFF) - HALF_RANGE).astype(I16)


def _count16(ref16, rs, nblk, tk, pred):
    strip = rs.stop - rs.start

    def body(kb, acc):
        col0 = pl.multiple_of(kb * tk, tk)
        for c in range(tk // LANES):
            chunk = ref16[rs, pl.ds(col0 + c * LANES, LANES)]
            acc = acc + jnp.where(pred(chunk), jnp.int16(1), jnp.int16(0))
        return acc

    acc = lax.fori_loop(0, nblk, body, jnp.zeros((strip, LANES), I16))
    return jnp.broadcast_to(jnp.sum(acc.astype(I32), axis=1, keepdims=True), (strip, LANES))


def _kth_largest16(ref16, rs, nblk, tk, k):
    strip = rs.stop - rs.start

    def bit_body(i, ans):
        cand = ans + jnp.left_shift(jnp.int32(1), 15 - i)
        cand16 = cand.astype(I16)
        cnt = _count16(ref16, rs, nblk, tk, lambda chunk: chunk >= cand16)
        return jnp.where(cnt >= k, cand, ans)

    return lax.fori_loop(0, 16, bit_body, jnp.full((strip, LANES), -HALF_RANGE, I32))


def _kth_largest_split(hi_ref, lo_ref, rs, nblk, tk, k):
    hi = _kth_largest16(hi_ref, rs, nblk, tk, k)
    hi16 = hi.astype(I16)
    k_lo = k - _count16(hi_ref, rs, nblk, tk, lambda chunk: chunk > hi16)

    def mask_body(kb, carry):
        cols = pl.ds(pl.multiple_of(kb * tk, tk), tk)
        reps = tk // LANES
        same = hi_ref[rs, cols] == jnp.concatenate([hi16] * reps, axis=1)
        lo_ref[rs, cols] = jnp.where(same, lo_ref[rs, cols], jnp.int16(-HALF_RANGE))
        return carry

    lax.fori_loop(0, nblk, mask_body, 0)
    lo = _kth_largest16(lo_ref, rs, nblk, tk, k_lo)
    return hi * (2 * HALF_RANGE) + (lo + HALF_RANGE)


def _topk_threshold(keys_ref, thr_ref, jmax_ref, nblk, tk, k, row_ok=None, hi_ref=None, lo_ref=None):
    rows = keys_ref.shape[0]
    col_bits = int(keys_ref.shape[1]).bit_length()
    strip = min(rows, COUNT_STRIP)
    lane = lax.broadcasted_iota(I32, (strip, LANES), 1)

    for r0 in range(0, rows, strip):
        rs = slice(r0, r0 + strip)

        def bit_body(i, ans, rs=rs):
            cand = ans + jnp.left_shift(jnp.int32(1), 31 - i)
            cnt = _count_rows(keys_ref, rs, nblk, tk, lambda chunk, col0: chunk >= cand)
            return jnp.where(cnt >= k, cand, ans)

        def digit_body(i, ans, rs=rs):
            step = jnp.left_shift(jnp.int32(1), 30 - 2 * i)
            cands = [ans + step * m for m in (1, 2, 3)]
            cnts = _count_rows(keys_ref, rs, nblk, tk,
                               [lambda chunk, col0, c=c: chunk >= c for c in cands])
            for c, n in zip(cands, cnts):
                ans = jnp.where(n >= k, c, ans)
            return ans

        if hi_ref is None and strip <= 16:
            thr = lax.fori_loop(0, 16, digit_body, jnp.full((strip, LANES), INT_MIN, I32))
        elif hi_ref is None:
            thr = lax.fori_loop(0, 32, bit_body, jnp.full((strip, LANES), INT_MIN, I32))
        else:
            thr = _kth_largest_split(hi_ref, lo_ref, rs, nblk, tk, k)
        n_gt = _count_rows(keys_ref, rs, nblk, tk, lambda chunk, col0: chunk > thr)
        n_eq = _count_rows(keys_ref, rs, nblk, tk, lambda chunk, col0: chunk == thr)
        need = k - n_gt
        live = thr != INT_MIN
        partial = live & (n_eq != need)
        if row_ok is not None:
            partial = partial & row_ok

        def tie_search(rs=rs, thr=thr, need=need):
            def jbody(i, j):
                cand = j + jnp.left_shift(jnp.int32(1), col_bits - 1 - i)
                cnt = _count_rows(keys_ref, rs, nblk, tk,
                                  lambda chunk, col0: (chunk == thr) & (lane < cand - col0))
                return jnp.where(cnt < need, cand, j)

            return lax.fori_loop(0, col_bits, jbody, jnp.zeros((strip, LANES), I32))

        any_partial = jnp.max(jnp.where(partial, 1.0, 0.0)) > 0.5
        jsearch = lax.cond(any_partial, tie_search, lambda: jnp.zeros((strip, LANES), I32))
        thr_ref[rs] = thr
        jmax_ref[rs] = jnp.where(live, jnp.where(partial, jsearch, jnp.int32(2 ** 30)), -1)


def _select_bias(keys_ref, rows, col0, width, thr, jmax):
    lane = lax.broadcasted_iota(I32, thr.shape, 1)
    out = []
    for c in range(width // LANES):
        key = keys_ref[rows, pl.ds(col0 + c * LANES, LANES)]
        sel = (key > thr) | ((key == thr) & (lane <= jmax - (col0 + c * LANES)))
        out.append(jnp.where(sel, 0.0, NEG_BIG))
    return out[0] if len(out) == 1 else jnp.concatenate(out, axis=1)


def _head_out_pairs(o_heads):
    pieces = []
    lane = lax.broadcasted_iota(I32, o_heads[0].shape, 1)
    group = N_HEADS_A // N_KV_A
    for j in range(N_HEADS_A // 2):
        a, b = o_heads[2 * j], o_heads[2 * j + 1]
        if (2 * j) // group == 0:
            pieces.append(jnp.where(lane < HEAD_DIM, a, pltpu.roll(b, HEAD_DIM, 1)))
        else:
            pieces.append(jnp.where(lane < HEAD_DIM, pltpu.roll(a, HEAD_DIM, 1), b))
    return jnp.concatenate(pieces, axis=1)


def _prompt_a_kernel(qi_ref, wi_ref, qa_ref, ki2_ref, ka_ref, va_ref, o_ref,
                     keys_ref, hi_ref, lo_ref, thr_ref, jmax_ref, m_ref, l_ref, acc_ref,
                     *, tq, tk, ta, topk):
    qblk = pl.program_id(1)
    row = qblk * tq + lax.broadcasted_iota(I32, (tq, tk), 0)
    nblk = (qblk * tq + tq + tk - 1) // tk
    wv = wi_ref[0] * IDX_SCALE

    def score_body(kb, carry):
        col0 = pl.multiple_of(kb * tk, tk)
        kk = ki2_ref[0, pl.ds(col0, tk), :]
        k_lo, k_hi = kk[:, :LANES], kk[:, LANES:]
        score = jnp.zeros((tq, tk), F32)
        for j in range(N_IDX_HEADS // 2):
            qp = qi_ref[0, j]
            w0 = wv[:, IDX_DIM + 2 * j:IDX_DIM + 2 * j + 1]
            w1 = wv[:, IDX_DIM + 2 * j + 1:IDX_DIM + 2 * j + 2]
            score = score + w0 * jnp.maximum(_nt(qp, k_lo), 0.0)
            score = score + w1 * jnp.maximum(_nt(qp, k_hi), 0.0)
        col = col0 + lax.broadcasted_iota(I32, (tq, tk), 1)
        _store_keys(keys_ref, hi_ref, lo_ref, slice(None), pl.ds(col0, tk),
                    jnp.where(col <= row, _sortable(score), INT_MIN))
        return carry

    lax.fori_loop(0, nblk, score_body, 0)
    nblk_a = (qblk * tq + tq + ta - 1) // ta

    def pad_body(kb, carry):
        _store_keys(keys_ref, hi_ref, lo_ref, slice(None), pl.ds(pl.multiple_of(kb * tk, tk), tk),
                    jnp.full((tq, tk), INT_MIN, I32))
        return carry

    lax.fori_loop(nblk, nblk_a * (ta // tk), pad_body, 0)
    _topk_threshold(keys_ref, thr_ref, jmax_ref, nblk_a, ta, topk, hi_ref=hi_ref, lo_ref=lo_ref)
    thr, jmax = thr_ref[...], jmax_ref[...]

    m_ref[...] = jnp.full(m_ref.shape, NEG_BIG, F32)
    l_ref[...] = jnp.zeros(l_ref.shape, F32)
    acc_ref[...] = jnp.zeros(acc_ref.shape, F32)

    def attn_body(kb, carry):
        col0 = pl.multiple_of(kb * ta, ta)
        bias = _select_bias(keys_ref, slice(None), col0, ta, thr, jmax)
        kblk = ka_ref[0, pl.ds(col0, ta), :]
        vblk = va_ref[0, pl.ds(col0, ta), :]
        for hh in range(N_HEADS_A):
            s = _nt(qa_ref[0, hh], kblk) + bias
            m_old = m_ref[hh]
            m_new = jnp.maximum(m_old, jnp.max(s, axis=1, keepdims=True))
            m_safe = jnp.where(m_new < 0.5 * NEG_BIG, 0.0, m_new)
            p = jnp.exp2(s - m_safe[:, :1])
            alpha = jnp.exp2(m_old - m_new)
            l_ref[hh] = alpha * l_ref[hh] + jnp.sum(p, axis=1, keepdims=True)
            acc_ref[hh] = alpha * acc_ref[hh] + jnp.dot(p.astype(BF16), vblk, preferred_element_type=F32)
            m_ref[hh] = m_new
        return carry

    lax.fori_loop(0, nblk_a, attn_body, 0)
    o_ref[0] = _head_out_pairs([acc_ref[hh] / l_ref[hh] for hh in range(N_HEADS_A)]).astype(o_ref.dtype)


def _prompt_group_a(qi, wi, qa, ki2, kab, vab):
    b, npair, s, _ = qi.shape
    tq = min(256, s)
    tk = min(512, s)
    ta = min(1024, s)
    topk = min(TOPK_MAX, s // 4)
    kern = functools.partial(_prompt_a_kernel, tq=tq, tk=tk, ta=ta, topk=topk)
    return pl.pallas_call(
        kern,
        grid=(b, s // tq),
        in_specs=[pl.BlockSpec((1, npair, tq, LANES), lambda bb, i: (bb, 0, i, 0)),
                  pl.BlockSpec((1, tq, LANES), lambda bb, i: (bb, i, 0)),
                  pl.BlockSpec((1, N_HEADS_A, tq, LANES), lambda bb, i: (bb, 0, i, 0)),
                  pl.BlockSpec((1, s, 2 * LANES), lambda bb, i: (bb, 0, 0)),
                  pl.BlockSpec((1, s, LANES), lambda bb, i: (bb, 0, 0)),
                  pl.BlockSpec((1, s, LANES), lambda bb, i: (bb, 0, 0))],
        out_specs=pl.BlockSpec((1, tq, N_HEADS_A * HEAD_DIM), lambda bb, i: (bb, i, 0)),
        out_shape=jax.ShapeDtypeStruct((b, s, N_HEADS_A * HEAD_DIM), BF16),
        scratch_shapes=[pltpu.VMEM((tq, s), I32),
                        pltpu.VMEM((tq, s), I16),
                        pltpu.VMEM((tq, s), I16),
                        pltpu.VMEM((tq, LANES), I32),
                        pltpu.VMEM((tq, LANES), I32),
                        pltpu.VMEM((N_HEADS_A, tq, LANES), F32),
                        pltpu.VMEM((N_HEADS_A, tq, LANES), F32),
                        pltpu.VMEM((N_HEADS_A, tq, LANES), F32)],
        compiler_params=_params(("arbitrary", "arbitrary")),
        name="prompt_group_a",
    )(qi, wi, qa, ki2, kab, vab)


def _log_sigmoid(z):
    return jnp.minimum(z, 0.0) - jnp.log(1.0 + jnp.exp(-jnp.abs(z)))


def _split3(x):
    hi = x.astype(BF16)
    r1 = x - hi.astype(F32)
    mid = r1.astype(BF16)
    lo = (r1 - mid.astype(F32)).astype(BF16)
    return hi, mid, lo


def _prompt_b_kernel(q_ref, k_ref, v_ref, o_ref, acc_ref, *, tq, tk):
    qblk = pl.program_id(2)
    q = q_ref[0]
    lane_q = lax.broadcasted_iota(I32, q.shape, 1)
    zero = jnp.zeros_like(q)
    qm = jnp.concatenate([jnp.where(lane_q < HEAD_DIM, q, zero), jnp.where(lane_q < HEAD_DIM, zero, q)], axis=0)
    row = qblk * tq + lax.broadcasted_iota(I32, (2 * tq, tk), 0) % tq
    jj = lax.broadcasted_iota(I32, (tk, tk), 0)
    ss = lax.broadcasted_iota(I32, (tk, tk), 1)
    tri = jnp.where(jj > ss, 1.0, 0.0).astype(BF16)
    kb0 = (qblk * tq + tq - 1) // tk
    acc_ref[...] = jnp.zeros(acc_ref.shape, F32)

    def cond(carry):
        kb, _, cmax = carry
        return (kb >= 0) & (cmax > SB_EXIT)

    def body(carry):
        kb, c, _ = carry
        col0 = pl.multiple_of(kb * tk, tk)
        z = _nt(qm, k_ref[0, pl.ds(col0, tk), :])
        valid = (col0 + lax.broadcasted_iota(I32, (2 * tq, tk), 1)) < row
        ls = _log_sigmoid(z)
        lk = jnp.where(valid, ls - z, 0.0)
        hi, mid, lo = _split3(lk)
        after = (jnp.dot(hi, tri, preferred_element_type=F32)
                 + jnp.dot(mid, tri, preferred_element_type=F32)
                 + jnp.dot(lo, tri, preferred_element_type=F32))
        a = jnp.where(valid, jnp.exp(ls + after + c), 0.0)
        acc_ref[...] += jnp.dot(a.astype(BF16), v_ref[0, pl.ds(col0, tk), :], preferred_element_type=F32)
        c = c + jnp.sum(lk, axis=1, keepdims=True)
        return kb - 1, c, jnp.max(c)

    lax.while_loop(cond, body, (kb0, jnp.zeros((2 * tq, 1), F32), jnp.float32(0.0)))

    lane = lax.broadcasted_iota(I32, (tq, LANES), 1)
    o_ref[0] = jnp.where(lane < HEAD_DIM, acc_ref[:tq], acc_ref[tq:]).astype(o_ref.dtype)


def _prompt_group_b(qb, kbb, vbb):
    b, s, _ = qb.shape
    tq = min(256, s)
    tk = min(256, s)
    npair = N_HEADS_B // 2
    kern = functools.partial(_prompt_b_kernel, tq=tq, tk=tk)
    return pl.pallas_call(
        kern,
        grid=(b, npair, s // tq),
        in_specs=[pl.BlockSpec((1, tq, LANES), lambda bb, j, i: (bb, i, j)),
                  pl.BlockSpec((1, s, LANES), lambda bb, j, i: (bb, 0, j)),
                  pl.BlockSpec((1, s, LANES), lambda bb, j, i: (bb, 0, j))],
        out_specs=pl.BlockSpec((1, tq, LANES), lambda bb, j, i: (bb, i, j)),
        out_shape=jax.ShapeDtypeStruct((b, s, N_HEADS_B * HEAD_DIM), BF16),
        scratch_shapes=[pltpu.VMEM((2 * tq, LANES), F32)],
        compiler_params=_params(("arbitrary", "arbitrary", "arbitrary")),
        name="prompt_group_b",
    )(qb, kbb, vbb)


def _sample_a_kernel(pt_ref, qi_ref, w_ref, qa_ref, kin_ref, kan_ref, van_ref,
                     cki_ref, cka_ref, cva_ref, o_ref,
                     ki_buf, ka_buf, va_buf, sems, keys_ref, thr_ref, jmax_ref, s_ref,
                     *, n_pages, n_new, topk, chunk):
    b = pl.program_id(0)
    past = n_pages * PAGE_SIZE
    tpad = keys_ref.shape[0]
    rows = qa_ref.shape[1]
    slot = b % 2

    def page_copies(seq, buf_slot, p):
        phys = pt_ref[seq, p]
        dst = pl.ds(pl.multiple_of(p * PAGE_SIZE, PAGE_SIZE), PAGE_SIZE)
        return (pltpu.make_async_copy(cki_ref.at[phys], ki_buf.at[buf_slot, :, dst], sems.at[buf_slot, 0]),
                pltpu.make_async_copy(cka_ref.at[phys], ka_buf.at[buf_slot, :, dst], sems.at[buf_slot, 1]),
                pltpu.make_async_copy(cva_ref.at[phys], va_buf.at[buf_slot, :, dst], sems.at[buf_slot, 2]))

    def start_seq(seq, buf_slot):
        def body(p, carry):
            for cp in page_copies(seq, buf_slot, p):
                cp.start()
            return carry

        lax.fori_loop(0, n_pages, body, 0)

    def wait_seq(seq, buf_slot):
        def body(p, carry):
            for cp in page_copies(seq, buf_slot, p):
                cp.wait()
            return carry

        lax.fori_loop(0, n_pages, body, 0)

    @pl.when(b == 0)
    def _():
        start_seq(0, 0)

    @pl.when(b + 1 < pl.num_programs(0))
    def _():
        start_seq(b + 1, 1 - slot)

    wait_seq(b, slot)

    qi = qi_ref[0]
    w = w_ref[0] * IDX_SCALE
    tok = lax.broadcasted_iota(I32, (tpad, LANES), 0)

    def scores_of(kt):
        d = jnp.maximum(jnp.dot(qi, kt, preferred_element_type=F32), 0.0) * w
        return jnp.sum(d.reshape(tpad, N_IDX_HEADS, kt.shape[1]), axis=1)

    def score_body(c, carry):
        col0 = pl.multiple_of(c * chunk, chunk)
        sc = scores_of(ki_buf[slot, :, pl.ds(col0, chunk)].astype(BF16))
        keys_ref[:, pl.ds(col0, chunk)] = _sortable(sc)
        return carry

    lax.fori_loop(0, past // chunk, score_body, 0)
    new_col = lax.broadcasted_iota(I32, (tpad, LANES), 1)
    new_ok = (new_col <= tok) & (new_col < n_new)
    keys_ref[:, past:past + LANES] = jnp.where(new_ok, _sortable(scores_of(kin_ref[0])), INT_MIN)

    width = past + LANES
    sel_tk = max(t for t in range(LANES, 2048 + 1, LANES) if width % t == 0)
    _topk_threshold(keys_ref, thr_ref, jmax_ref, width // sel_tk, sel_tk, topk, row_ok=tok < n_new)
    thr, jmax = thr_ref[...], jmax_ref[...]

    qa = qa_ref[0]

    def masked_scores(kt, col0):
        n = kt.shape[1]
        bias = _select_bias(keys_ref, slice(None), col0, n, thr, jmax)
        s = jnp.dot(qa, kt, preferred_element_type=F32).reshape(tpad, N_HEADS_A, n) + bias[:, None, :]
        return s.reshape(rows, n)

    def s_body(c, m):
        col0 = pl.multiple_of(c * chunk, chunk)
        s = masked_scores(ka_buf[slot, :, pl.ds(col0, chunk)].astype(BF16), col0)
        s_ref[:, pl.ds(col0, chunk)] = s
        return jnp.maximum(m, jnp.max(s, axis=1, keepdims=True))

    m = lax.fori_loop(0, past // chunk, s_body, jnp.full((rows, 1), NEG_BIG, F32))
    s_new = masked_scores(kan_ref[0], past)
    m = jnp.maximum(m, jnp.max(s_new, axis=1, keepdims=True))

    p_new = jnp.exp2(s_new - m)
    l0 = jnp.sum(p_new, axis=1, keepdims=True)
    acc0 = _nt(p_new.astype(BF16), van_ref[0])

    def pv_body(c, carry):
        l, acc = carry
        col0 = pl.multiple_of(c * chunk, chunk)
        p = jnp.exp2(s_ref[:, pl.ds(col0, chunk)] - m)
        acc = acc + _nt(p.astype(BF16), va_buf[slot, :, pl.ds(col0, chunk)].astype(BF16))
        return l + jnp.sum(p, axis=1, keepdims=True), acc

    l, acc = lax.fori_loop(0, past // chunk, pv_body, (l0, acc0))
    o_ref[0] = acc / l


def _sample_group_a(page_table, qi_r, w_r, qa_r, kin, kan, van, cki, cka, cva, n_new):
    db, n_pages = page_table.shape
    past = n_pages * PAGE_SIZE
    rows = qa_r.shape[1]
    tpad = rows // N_HEADS_A
    topk = min(TOPK_MAX, (past + n_new) // 4)
    chunk = min(1024, past)
    kern = functools.partial(_sample_a_kernel, n_pages=n_pages, n_new=n_new, topk=topk, chunk=chunk)
    grid_spec = pltpu.PrefetchScalarGridSpec(
        num_scalar_prefetch=1,
        grid=(db,),
        in_specs=[pl.BlockSpec((1, rows, IDX_DIM), lambda i, pt: (i, 0, 0)),
                  pl.BlockSpec((1, rows, 1), lambda i, pt: (i, 0, 0)),
                  pl.BlockSpec((1, rows, LANES), lambda i, pt: (i, 0, 0)),
                  pl.BlockSpec((1, IDX_DIM, LANES), lambda i, pt: (i, 0, 0)),
                  pl.BlockSpec((1, LANES, LANES), lambda i, pt: (i, 0, 0)),
                  pl.BlockSpec((1, LANES, LANES), lambda i, pt: (i, 0, 0)),
                  pl.BlockSpec(memory_space=pl.ANY),
                  pl.BlockSpec(memory_space=pl.ANY),
                  pl.BlockSpec(memory_space=pl.ANY)],
        out_specs=pl.BlockSpec((1, rows, LANES), lambda i, pt: (i, 0, 0)),
        scratch_shapes=[pltpu.VMEM((2, IDX_DIM, past), F32),
                        pltpu.VMEM((2, LANES, past), F32),
                        pltpu.VMEM((2, LANES, past), F32),
                        pltpu.SemaphoreType.DMA((2, 3)),
                        pltpu.VMEM((tpad, past + LANES), I32),
                        pltpu.VMEM((tpad, LANES), I32),
                        pltpu.VMEM((tpad, LANES), I32),
                        pltpu.VMEM((rows, past), F32)])
    return pl.pallas_call(
        kern,
        grid_spec=grid_spec,
        out_shape=jax.ShapeDtypeStruct((db, rows, LANES), F32),
        compiler_params=_params(("arbitrary",)),
        name="sample_group_a",
    )(page_table, qi_r, w_r, qa_r, kin, kan, van, cki, cka, cva)


def _sample_b_kernel(pt_ref, q_ref, kn_ref, vn_ref, ck_ref, cv_ref, o_ref,
                     k_buf, v_buf, sems, *, n_pages):
    b = pl.program_id(0)
    q = q_ref[0]
    rows = q.shape[0]
    jj = lax.broadcasted_iota(I32, (PAGE_SIZE, PAGE_SIZE), 0)
    ss = lax.broadcasted_iota(I32, (PAGE_SIZE, PAGE_SIZE), 1)
    tri = jnp.where(jj > ss, 1.0, 0.0).astype(BF16)
    last = n_pages - 1

    def copies(p, slot):
        phys = pt_ref[b, p]
        return (pltpu.make_async_copy(ck_ref.at[phys], k_buf.at[slot], sems.at[slot, 0]),
                pltpu.make_async_copy(cv_ref.at[phys], v_buf.at[slot], sems.at[slot, 1]))

    def block(kt, vt, valid, c):
        z = jnp.dot(q, kt, preferred_element_type=F32)
        ls = _log_sigmoid(z)
        lk = ls - z
        if valid is not None:
            lk = jnp.where(valid, lk, 0.0)
        hi, mid, lo = _split3(lk)
        after = (jnp.dot(hi, tri, preferred_element_type=F32)
                 + jnp.dot(mid, tri, preferred_element_type=F32)
                 + jnp.dot(lo, tri, preferred_element_type=F32))
        a = jnp.exp(ls + after + c)
        if valid is not None:
            a = jnp.where(valid, a, 0.0)
        return c + jnp.sum(lk, axis=1, keepdims=True), _nt(a.astype(BF16), vt)

    for cp in copies(last, 0):
        cp.start()

    tok = lax.broadcasted_iota(I32, (rows, PAGE_SIZE), 0) // N_HEADS_B
    valid_new = lax.broadcasted_iota(I32, (rows, PAGE_SIZE), 1) < tok
    c, o = block(kn_ref[0], vn_ref[0], valid_new, jnp.zeros((rows, 1), F32))

    def cond(carry):
        p, _, _, cmax = carry
        return (p >= 0) & (cmax > SB_EXIT)

    def body(carry):
        p, c, o, _ = carry
        slot = (last - p) % 2
        for cp in copies(p, slot):
            cp.wait()

        @pl.when(p > 0)
        def _():
            for cp in copies(p - 1, 1 - slot):
                cp.start()

        c, o_add = block(k_buf[slot].astype(BF16), v_buf[slot].astype(BF16), None, c)
        return p - 1, c, o + o_add, jnp.max(c)

    p_end, _, o, _ = lax.while_loop(cond, body, (jnp.int32(last), c, o, jnp.max(c)))

    @pl.when(p_end >= 0)
    def _():
        for cp in copies(p_end, (last - p_end) % 2):
            cp.wait()

    o_ref[0] = o


def _sample_group_b(page_table, q_bd, knt, vnt, ckb, cvb):
    db, n_pages = page_table.shape
    rows, width = q_bd.shape[1:]
    kern = functools.partial(_sample_b_kernel, n_pages=n_pages)
    grid_spec = pltpu.PrefetchScalarGridSpec(
        num_scalar_prefetch=1,
        grid=(db,),
        in_specs=[pl.BlockSpec((1, rows, width), lambda i, pt: (i, 0, 0)),
                  pl.BlockSpec((1, width, PAGE_SIZE), lambda i, pt: (i, 0, 0)),
                  pl.BlockSpec((1, width, PAGE_SIZE), lambda i, pt: (i, 0, 0)),
                  pl.BlockSpec(memory_space=pl.ANY),
                  pl.BlockSpec(memory_space=pl.ANY)],
        out_specs=pl.BlockSpec((1, rows, width), lambda i, pt: (i, 0, 0)),
        scratch_shapes=[pltpu.VMEM((2, width, PAGE_SIZE), F32),
                        pltpu.VMEM((2, width, PAGE_SIZE), F32),
                        pltpu.SemaphoreType.DMA((2, 2))])
    return pl.pallas_call(
        kern,
        grid_spec=grid_spec,
        out_shape=jax.ShapeDtypeStruct((db, rows, width), F32),
        compiler_params=_params(("arbitrary",)),
        name="sample_group_b",
    )(page_table, q_bd, knt, vnt, ckb, cvb)


def _layernorm(x, g, b):
    mu = jnp.mean(x, axis=-1, keepdims=True)
    xc = x - mu
    var = jnp.mean(xc * xc, axis=-1, keepdims=True)
    return xc * lax.rsqrt(var + LN_EPS) * g + b


def _attn_out_kernel(x_ref, oa_ref, ob_ref, gate_ref, wo_ref, g_ref, b_ref, o_ref, *, alpha):
    half = oa_ref.shape[2]
    mix = (jnp.dot(oa_ref[0], wo_ref[:half, :], preferred_element_type=F32)
           + jnp.dot(ob_ref[0], wo_ref[half:, :], preferred_element_type=F32))
    o_ref[0] = _layernorm(alpha * x_ref[0] + gate_ref[0] * mix, g_ref[...], b_ref[...])


def _attn_out(x, oa, ob, gate, w_o, g, bta, alpha):
    gg, r, d = x.shape
    tm = min(512, r)
    rm = gate.shape[1]
    mod_blk = (1, 1, d) if rm == 1 else (1, tm, d)
    mod_map = (lambda b, i: (b, 0, 0)) if rm == 1 else (lambda b, i: (b, i, 0))
    wa = oa.shape[2]
    return pl.pallas_call(
        functools.partial(_attn_out_kernel, alpha=alpha),
        grid=(gg, r // tm),
        in_specs=[pl.BlockSpec((1, tm, d), lambda b, i: (b, i, 0)),
                  pl.BlockSpec((1, tm, wa), lambda b, i: (b, i, 0)),
                  pl.BlockSpec((1, tm, wa), lambda b, i: (b, i, 0)),
                  pl.BlockSpec(mod_blk, mod_map),
                  pl.BlockSpec((2 * wa, d), lambda b, i: (0, 0)),
                  pl.BlockSpec((1, d), lambda b, i: (0, 0)),
                  pl.BlockSpec((1, d), lambda b, i: (0, 0))],
        out_specs=pl.BlockSpec((1, tm, d), lambda b, i: (b, i, 0)),
        out_shape=jax.ShapeDtypeStruct((gg, r, d), F32),
        compiler_params=_params(("arbitrary", "arbitrary")),
        name="attn_out_ln",
    )(x, oa, ob, gate, w_o, g.reshape(1, d), bta.reshape(1, d))


def _ffn_kernel(x_ref, shift_ref, scale_ref, gate_ref, wu_ref, wd_ref, g_ref, b_ref, o_ref,
                h_ref, acc_ref, *, alpha):
    f = pl.program_id(2)

    @pl.when(f == 0)
    def _():
        h_ref[...] = (x_ref[0] * (1.0 + scale_ref[0]) + shift_ref[0]).astype(BF16)
        acc_ref[...] = jnp.zeros(acc_ref.shape, F32)

    u = jnp.maximum(jnp.dot(h_ref[...], wu_ref[...], preferred_element_type=F32), 0.0)
    acc_ref[...] += jnp.dot((u * u).astype(BF16), wd_ref[...], preferred_element_type=F32)

    @pl.when(f == pl.num_programs(2) - 1)
    def _():
        o_ref[0] = _layernorm(alpha * x_ref[0] + gate_ref[0] * acc_ref[...], g_ref[...], b_ref[...])


def _ffn(x, shift, scale, gate, w_up, w_down, g, bta, alpha):
    gg, r, d = x.shape
    dff = w_up.shape[1]
    tm = min(512, r)
    tf = min(1024, dff)
    rm = gate.shape[1]
    mod_blk = (1, 1, d) if rm == 1 else (1, tm, d)
    mod_map = (lambda b, i, f: (b, 0, 0)) if rm == 1 else (lambda b, i, f: (b, i, 0))
    return pl.pallas_call(
        functools.partial(_ffn_kernel, alpha=alpha),
        grid=(gg, r // tm, dff // tf),
        in_specs=[pl.BlockSpec((1, tm, d), lambda b, i, f: (b, i, 0)),
                  pl.BlockSpec(mod_blk, mod_map),
                  pl.BlockSpec(mod_blk, mod_map),
                  pl.BlockSpec(mod_blk, mod_map),
                  pl.BlockSpec((d, tf), lambda b, i, f: (0, f)),
                  pl.BlockSpec((tf, d), lambda b, i, f: (f, 0)),
                  pl.BlockSpec((1, d), lambda b, i, f: (0, 0)),
                  pl.BlockSpec((1, d), lambda b, i, f: (0, 0))],
        out_specs=pl.BlockSpec((1, tm, d), lambda b, i, f: (b, i, 0)),
        out_shape=jax.ShapeDtypeStruct((gg, r, d), F32),
        scratch_shapes=[pltpu.VMEM((tm, d), BF16), pltpu.VMEM((tm, d), F32)],
        compiler_params=_params(("arbitrary", "arbitrary", "arbitrary")),
        name="ffn_ln",
    )(x, shift, scale, gate, w_up, w_down, g.reshape(1, d), bta.reshape(1, d))


def _rope_tables(pos):
    inv = 1.0 / (ROPE_THETA ** (jnp.arange(HALF, dtype=F32) * (2.0 / HEAD_DIM)))
    ang = pos.astype(F32)[:, None] * inv[None, :]
    cos = jnp.tile(jnp.cos(ang), (1, 2 * LANES // HEAD_DIM))
    sin = jnp.sin(ang)
    sin = jnp.tile(jnp.concatenate([-sin, sin], axis=1), (1, LANES // HEAD_DIM))
    return cos, sin


def _pad_w_in(w_in_l):
    d = w_in_l.shape[0]
    group = N_HEADS_A // N_KV_A
    o_qa, o_ka, o_va = 0, 512, 640
    o_qi, o_ki, o_wi, o_qb, o_kb, o_vb = 768, 1280, 1344, 1352, 1864, 2376
    cols = []
    zero = jnp.zeros((d, HEAD_DIM), w_in_l.dtype)
    for hh in range(N_HEADS_A):
        wq = w_in_l[:, o_qa + hh * HEAD_DIM:o_qa + (hh + 1) * HEAD_DIM]
        cols += [wq, zero] if hh // group == 0 else [zero, wq]
    cols.append(w_in_l[:, o_ka:o_qi])
    cols.append(w_in_l[:, o_qi:o_ki])
    cols.append(w_in_l[:, o_ki:o_qb])
    cols.append(jnp.zeros((d, LANES - IDX_DIM - N_IDX_HEADS), w_in_l.dtype))
    cols.append(w_in_l[:, o_qb:])
    w = jnp.concatenate(cols, axis=1)
    assert w.shape[1] == _W_TOT
    return w.astype(BF16)


def kernel(x_prompt, x_sample, cache_k_a, cache_v_a, cache_k_idx, cache_k_b, cache_v_b, page_table,
           c_prompt, c_sample, w_mod, b_mod, w_in, w_o, ln1_g, ln1_b, w_up, w_down, ln2_g, ln2_b):
    depth = w_in.shape[0]
    alpha = float((2.0 * depth) ** 0.25)
    bsz, seq, d = x_prompt.shape
    db, t_new, _ = x_sample.shape
    n_pages = page_table.shape[1]
    past = n_pages * PAGE_SIZE
    n_phys = cache_k_a.shape[1]
    tpad = 8

    def page_major(cache):
        nd = cache.ndim
        t = jnp.transpose(cache, (0, 1) + tuple(range(3, nd)) + (2,))
        return t.reshape(depth, n_phys, -1, PAGE_SIZE)

    cki_t, cka_t, cva_t = page_major(cache_k_idx), page_major(cache_k_a), page_major(cache_v_a)
    ckb_t, cvb_t = page_major(cache_k_b), page_major(cache_v_b)

    cos_p, sin_p = _rope_tables(jnp.arange(seq, dtype=I32))
    cos_s, sin_s = _rope_tables(past + jnp.arange(t_new, dtype=I32))
    cos_s = jnp.tile(cos_s, (db, 1))
    sin_s = jnp.tile(sin_s, (db, 1))

    yp = x_prompt
    ys = x_sample.reshape(1, db * t_new, d)
    rows_p, rows_s = [], []
    for l in range(depth):
        w_pad = _pad_w_in(w_in[l])
        wo_b, wu_b, wd_b = w_o[l].astype(BF16), w_up[l].astype(BF16), w_down[l].astype(BF16)

        mod_p = _modulation(c_prompt, w_mod[l], b_mod[l])[:, None, :]
        sh_m, sc_m, g_m, sh_f, sc_f, g_f = jnp.split(mod_p, 6, axis=-1)
        (qa, ka, va, qi, ki, wi, qb, kb, vb, kab, vab, ki2, kbb, vbb) = _project(
            yp, sh_m, sc_m, cos_p, sin_p, w_pad)
        oa = _prompt_group_a(qi, wi, qa, ki2, kab, vab)
        ob = _prompt_group_b(qb, kbb, vbb)
        x1 = _attn_out(yp, oa, ob, g_m, wo_b, ln1_g[l], ln1_b[l], alpha)
        yp = _ffn(x1, sh_f, sc_f, g_f, wu_b, wd_b, ln2_g[l], ln2_b[l], alpha)
        rows_p.append((ka.reshape(bsz, seq, N_KV_A, HEAD_DIM), va.reshape(bsz, seq, N_KV_A, HEAD_DIM), ki,
                       kb.reshape(bsz, seq, N_HEADS_B, HEAD_DIM), vb.reshape(bsz, seq, N_HEADS_B, HEAD_DIM)))

        mod_s = _modulation(c_sample, w_mod[l], b_mod[l])
        mod_s = jnp.broadcast_to(mod_s[:, None, :], (db, t_new, 6 * d)).reshape(1, db * t_new, 6 * d)
        sh_m, sc_m, g_m, sh_f, sc_f, g_f = jnp.split(mod_s, 6, axis=-1)
        (qa, ka, va, qi, ki, wi, qb, kb, vb, kab, vab, ki2, kbb, vbb) = _project(
            ys, sh_m, sc_m, cos_s, sin_s, w_pad)

        def pad_tokens(a, n):
            return jnp.pad(a, [(0, 0), (0, n - a.shape[1])] + [(0, 0)] * (a.ndim - 2))

        qi_r = qi[0].reshape(N_IDX_HEADS // 2, db, t_new, 2, IDX_DIM).transpose(1, 2, 0, 3, 4)
        qi_r = pad_tokens(qi_r.reshape(db, t_new, N_IDX_HEADS, IDX_DIM), tpad)
        qi_r = qi_r.reshape(db, tpad * N_IDX_HEADS, IDX_DIM)
        w_r = pad_tokens(wi.reshape(db, t_new, LANES)[:, :, IDX_DIM:IDX_DIM + N_IDX_HEADS], tpad)
        w_r = w_r.reshape(db, tpad * N_IDX_HEADS, 1)
        qa_r = qa[0].reshape(N_HEADS_A, db, t_new, LANES).transpose(1, 2, 0, 3)
        qa_r = pad_tokens(qa_r, tpad).reshape(db, tpad * N_HEADS_A, LANES)

        def new_page(a):
            a = jnp.swapaxes(a.reshape(db, t_new, a.shape[-1]), 1, 2)
            return jnp.pad(a, [(0, 0), (0, 0), (0, PAGE_SIZE - t_new)]).astype(BF16)

        oa_r = _sample_group_a(page_table, qi_r, w_r, qa_r, new_page(ki), new_page(kab), new_page(vab),
                               cki_t[l], cka_t[l], cva_t[l], t_new)
        oa_r = oa_r.reshape(db, tpad, N_KV_A, N_HEADS_A // N_KV_A, N_KV_A, HEAD_DIM)[:, :t_new]
        oa_s = jnp.stack([oa_r[:, :, n, :, n, :] for n in range(N_KV_A)], axis=2)
        oa_s = oa_s.reshape(1, db * t_new, N_HEADS_A * HEAD_DIM).astype(BF16)

        eye = jnp.eye(N_HEADS_B, dtype=BF16)
        q_bd = jnp.einsum('bthd,gh->btghd', qb.reshape(db, t_new, N_HEADS_B, HEAD_DIM), eye)
        q_bd = q_bd.reshape(db, t_new * N_HEADS_B, N_HEADS_B * HEAD_DIM)
        ob_r = _sample_group_b(page_table, q_bd, new_page(kbb), new_page(vbb), ckb_t[l], cvb_t[l])
        ob_r = ob_r.reshape(db, t_new, N_HEADS_B, N_HEADS_B, HEAD_DIM)
        ob_s = jnp.stack([ob_r[:, :, hh, hh, :] for hh in range(N_HEADS_B)], axis=2)
        ob_s = ob_s.reshape(1, db * t_new, N_HEADS_B * HEAD_DIM).astype(BF16)

        x1 = _attn_out(ys, oa_s, ob_s, g_m, wo_b, ln1_g[l], ln1_b[l], alpha)
        ys = _ffn(x1, sh_f, sc_f, g_f, wu_b, wd_b, ln2_g[l], ln2_b[l], alpha)
        rows_s.append((ka.reshape(db, t_new, N_KV_A, HEAD_DIM), va.reshape(db, t_new, N_KV_A, HEAD_DIM),
                       ki.reshape(db, t_new, IDX_DIM),
                       kb.reshape(db, t_new, N_HEADS_B, HEAD_DIM), vb.reshape(db, t_new, N_HEADS_B, HEAD_DIM)))

    outs_p = [jnp.stack([r[i] for r in rows_p]) for i in range(5)]
    outs_s = [jnp.stack([r[i] for r in rows_s]) for i in range(5)]
    return (yp, ys.reshape(db, t_new, d), *outs_p, *outs_s)
```

```python
import functools

import numpy as np
import jax
import jax.numpy as jnp
from jax import lax
from jax.experimental import pallas as pl
from jax.experimental.pallas import tpu as pltpu

F32 = jnp.float32
BF16 = jnp.bfloat16
I32 = jnp.int32

HEAD_DIM = 64
N_HEADS_A = 8
N_KV_A = 2
N_HEADS_B = 8
N_IDX_HEADS = 8
IDX_DIM = 64
TOPK_MAX = 256
PAGE_SIZE = 128
ROPE_THETA = 10000.0
LN_EPS = 1e-5
LANES = 128
HALF = HEAD_DIM // 2
QK_SCALE = HEAD_DIM ** -0.5
QK_SCALE_LOG2 = QK_SCALE * 1.4426950408889634
IDX_SCALE = (IDX_DIM ** -0.5) * (N_IDX_HEADS ** -0.5)
INT_MIN = -(2 ** 31)
NEG_BIG = -1e30
SB_EXIT = -110.0
COUNT_STRIP = 128
VMEM_LIMIT = 56 * 1024 * 1024

_W_QA = N_HEADS_A * LANES
_W_KA = N_KV_A * HEAD_DIM
_W_QI = N_IDX_HEADS * IDX_DIM
_W_B = N_HEADS_B * HEAD_DIM
_OFF_QA = 0
_OFF_KA = _OFF_QA + _W_QA
_OFF_VA = _OFF_KA + _W_KA
_OFF_QI = _OFF_VA + _W_KA
_OFF_KW = _OFF_QI + _W_QI
_OFF_QB = _OFF_KW + LANES
_OFF_KB = _OFF_QB + _W_B
_OFF_VB = _OFF_KB + _W_B
_W_TOT = _OFF_VB + _W_B


def _nt(a, b):
    return lax.dot_general(a, b, (((1,), (1,)), ((), ())), preferred_element_type=F32)


def _params(sem):
    return pltpu.CompilerParams(dimension_semantics=sem, vmem_limit_bytes=VMEM_LIMIT)


def _mod_kernel(c_ref, w_ref, b_ref, o_ref):
    o_ref[...] = jnp.dot(c_ref[...], w_ref[...], preferred_element_type=F32,
                         precision=lax.Precision.HIGHEST) + b_ref[...]


def _modulation(c, w_mod, b_mod):
    r, d = c.shape
    n = w_mod.shape[1]
    tn = 1536 if n % 1536 == 0 else n
    return pl.pallas_call(
        _mod_kernel,
        grid=(n // tn,),
        in_specs=[pl.BlockSpec((r, d), lambda j: (0, 0)),
                  pl.BlockSpec((d, tn), lambda j: (0, j)),
                  pl.BlockSpec((1, tn), lambda j: (0, j))],
        out_specs=pl.BlockSpec((r, tn), lambda j: (0, j)),
        out_shape=jax.ShapeDtypeStruct((r, n), F32),
        compiler_params=_params(("arbitrary",)),
        name="modulation",
    )(c, w_mod, b_mod.reshape(1, n))


def _rope(x, cos, sin_signed):
    w = x.shape[1]
    lane = lax.broadcasted_iota(I32, x.shape, 1)
    partner = jnp.where(lane % HEAD_DIM < HALF, pltpu.roll(x, w - HALF, 1), pltpu.roll(x, HALF, 1))
    reps = w // LANES
    if reps > 1:
        cos = jnp.concatenate([cos] * reps, axis=1)
        sin_signed = jnp.concatenate([sin_signed] * reps, axis=1)
    return x * cos + partner * sin_signed


def _proj_kernel(x_ref, shift_ref, scale_ref, cos_ref, sin_ref, w_ref,
                 qa_ref, ka_ref, va_ref, qi_ref, ki_ref, wi_ref, qb_ref, kb_ref, vb_ref,
                 kab_ref, vab_ref, ki2_ref, kbb_ref, vbb_ref):
    h = (x_ref[0] * (1.0 + scale_ref[0]) + shift_ref[0]).astype(BF16)
    cos = cos_ref[...]
    sin = sin_ref[...]

    def mm(off, width):
        return jnp.dot(h, w_ref[:, off:off + width], preferred_element_type=F32)

    for hh in range(N_HEADS_A):
        q = _rope(mm(_OFF_QA + hh * LANES, LANES), cos, sin) * QK_SCALE_LOG2
        qa_ref[0, hh] = q.astype(BF16)
    ka = _rope(mm(_OFF_KA, _W_KA), cos, sin)
    ka_ref[0] = ka
    kab_ref[0] = ka.astype(BF16)
    va = mm(_OFF_VA, _W_KA)
    va_ref[0] = va
    vab_ref[0] = va.astype(BF16)
    qi = _rope(mm(_OFF_QI, _W_QI), cos, sin).astype(BF16)
    for j in range(N_IDX_HEADS // 2):
        qi_ref[0, j] = qi[:, j * LANES:(j + 1) * LANES]
    kw = mm(_OFF_KW, LANES)
    lane = lax.broadcasted_iota(I32, kw.shape, 1)
    ki = jnp.where(lane < IDX_DIM, _rope(kw, cos, sin), 0.0)
    ki_ref[0] = ki[:, :IDX_DIM]
    wi_ref[0] = kw
    kib = ki.astype(BF16)
    ki2_ref[0, :, :LANES] = kib
    ki2_ref[0, :, LANES:] = pltpu.roll(ki, IDX_DIM, 1).astype(BF16)
    qb_ref[0] = (mm(_OFF_QB, _W_B) * QK_SCALE).astype(BF16)
    kb = mm(_OFF_KB, _W_B)
    kb_ref[0] = kb
    kbb_ref[0] = kb.astype(BF16)
    vb = mm(_OFF_VB, _W_B)
    vb_ref[0] = vb
    vbb_ref[0] = vb.astype(BF16)


def _project(x, shift, scale, cos, sin, w_pad):
    g, r, d = x.shape
    tm = min(512, r)
    rm = shift.shape[1]
    mod_blk = (1, 1, d) if rm == 1 else (1, tm, d)
    mod_map = (lambda b, i: (b, 0, 0)) if rm == 1 else (lambda b, i: (b, i, 0))

    def out(width, dtype):
        return (pl.BlockSpec((1, tm, width), lambda b, i: (b, i, 0)),
                jax.ShapeDtypeStruct((g, r, width), dtype))

    def out_heads(n):
        return (pl.BlockSpec((1, n, tm, LANES), lambda b, i: (b, 0, i, 0)),
                jax.ShapeDtypeStruct((g, n, r, LANES), BF16))

    outs = [out_heads(N_HEADS_A), out(_W_KA, F32), out(_W_KA, F32), out_heads(N_IDX_HEADS // 2),
            out(IDX_DIM, F32), out(LANES, F32), out(_W_B, BF16), out(_W_B, F32), out(_W_B, F32),
            out(_W_KA, BF16), out(_W_KA, BF16), out(2 * LANES, BF16), out(_W_B, BF16), out(_W_B, BF16)]
    return pl.pallas_call(
        _proj_kernel,
        grid=(g, r // tm),
        in_specs=[pl.BlockSpec((1, tm, d), lambda b, i: (b, i, 0)),
                  pl.BlockSpec(mod_blk, mod_map),
                  pl.BlockSpec(mod_blk, mod_map),
                  pl.BlockSpec((tm, LANES), lambda b, i: (i, 0)),
                  pl.BlockSpec((tm, LANES), lambda b, i: (i, 0)),
                  pl.BlockSpec((d, _W_TOT), lambda b, i: (0, 0))],
        out_specs=[o[0] for o in outs],
        out_shape=[o[1] for o in outs],
        compiler_params=_params(("arbitrary", "arbitrary")),
        name="project",
    )(x, shift, scale, cos, sin, w_pad)


def _sortable(score):
    score = jnp.where(score == 0.0, 0.0, score)
    bits = pltpu.bitcast(score, I32)
    return jnp.where(bits < 0, bits ^ 0x7FFFFFFF, bits)


def _count_rows(keys_ref, rs, nblk, tk, pred):
    strip = rs.stop - rs.start
    preds = pred if isinstance(pred, (list, tuple)) else [pred]

    def body(kb, accs):
        col0 = pl.multiple_of(kb * tk, tk)
        accs = list(accs)
        for c in range(tk // LANES):
            chunk = keys_ref[rs, pl.ds(col0 + c * LANES, LANES)]
            for n, p in enumerate(preds):
                accs[n] = accs[n] + jnp.where(p(chunk, col0 + c * LANES), 1, 0)
        return tuple(accs)

    accs = lax.fori_loop(0, nblk, body, tuple(jnp.zeros((strip, LANES), I32) for _ in preds))
    cnts = [jnp.broadcast_to(jnp.sum(a, axis=1, keepdims=True), (strip, LANES)) for a in accs]
    return cnts if isinstance(pred, (list, tuple)) else cnts[0]


def _topk_threshold(keys_ref, thr_ref, jmax_ref, nblk, tk, k, row_ok=None):
    rows = keys_ref.shape[0]
    col_bits = int(keys_ref.shape[1]).bit_length()
    strip = min(rows, COUNT_STRIP)
    lane = lax.broadcasted_iota(I32, (strip, LANES), 1)

    for r0 in range(0, rows, strip):
        rs = slice(r0, r0 + strip)

        def bit_body(i, ans, rs=rs):
            cand = ans + jnp.left_shift(jnp.int32(1), 31 - i)
            cnt = _count_rows(keys_ref, rs, nblk, tk, lambda chunk, col0: chunk >= cand)
            return jnp.where(cnt >= k, cand, ans)

        def digit_body(i, ans, rs=rs):
            step = jnp.left_shift(jnp.int32(1), 30 - 2 * i)
            cands = [ans + step * m for m in (1, 2, 3)]
            cnts = _count_rows(keys_ref, rs, nblk, tk,
                               [lambda chunk, col0, c=c: chunk >= c for c in cands])
            for c, n in zip(cands, cnts):
                ans = jnp.where(n >= k, c, ans)
            return ans

        if strip <= 16:
            thr = lax.fori_loop(0, 16, digit_body, jnp.full((strip, LANES), INT_MIN, I32))
        else:
            thr = lax.fori_loop(0, 32, bit_body, jnp.full((strip, LANES), INT_MIN, I32))
        n_gt = _count_rows(keys_ref, rs, nblk, tk, lambda chunk, col0: chunk > thr)
        n_eq = _count_rows(keys_ref, rs, nblk, tk, lambda chunk, col0: chunk == thr)
        need = k - n_gt
        live = thr != INT_MIN
        partial = live & (n_eq != need)
        if row_ok is not None:
            partial = partial & row_ok

        def tie_search(rs=rs, thr=thr, need=need):
            def jbody(i, j):
                cand = j + jnp.left_shift(jnp.int32(1), col_bits - 1 - i)
                cnt = _count_rows(keys_ref, rs, nblk, tk,
                                  lambda chunk, col0: (chunk == thr) & (lane < cand - col0))
                return jnp.where(cnt < need, cand, j)

            return lax.fori_loop(0, col_bits, jbody, jnp.zeros((strip, LANES), I32))

        any_partial = jnp.max(jnp.where(partial, 1.0, 0.0)) > 0.5
        jsearch = lax.cond(any_partial, tie_search, lambda: jnp.zeros((strip, LANES), I32))
        thr_ref[rs] = thr
        jmax_ref[rs] = jnp.where(live, jnp.where(partial, jsearch, jnp.int32(2 ** 30)), -1)


def _select_bias(keys_ref, rows, col0, width, thr, jmax):
    lane = lax.broadcasted_iota(I32, thr.shape, 1)
    out = []
    for c in range(width // LANES):
        key = keys_ref[rows, pl.ds(col0 + c * LANES, LANES)]
        sel = (key > thr) | ((key == thr) & (lane <= jmax - (col0 + c * LANES)))
        out.append(jnp.where(sel, 0.0, NEG_BIG))
    return out[0] if len(out) == 1 else jnp.concatenate(out, axis=1)


def _head_out_pairs(o_heads):
    pieces = []
    lane = lax.broadcasted_iota(I32, o_heads[0].shape, 1)
    group = N_HEADS_A // N_KV_A
    for j in range(N_HEADS_A // 2):
        a, b = o_heads[2 * j], o_heads[2 * j + 1]
        if (2 * j) // group == 0:
            pieces.append(jnp.where(lane < HEAD_DIM, a, pltpu.roll(b, HEAD_DIM, 1)))
        else:
            pieces.append(jnp.where(lane < HEAD_DIM, pltpu.roll(a, HEAD_DIM, 1), b))
    return jnp.concatenate(pieces, axis=1)


def _prompt_a_kernel(qi_ref, wi_ref, qa_ref, ki2_ref, ka_ref, va_ref, o_ref,
                     keys_ref, thr_ref, jmax_ref, m_ref, l_ref, acc_ref, *, tq, tk, ta, topk):
    qblk = pl.program_id(1)
    row = qblk * tq + lax.broadcasted_iota(I32, (tq, tk), 0)
    nblk = (qblk * tq + tq + tk - 1) // tk
    wv = wi_ref[0] * IDX_SCALE

    def score_body(kb, carry):
        col0 = pl.multiple_of(kb * tk, tk)
        kk = ki2_ref[0, pl.ds(col0, tk), :]
        k_lo, k_hi = kk[:, :LANES], kk[:, LANES:]
        score = jnp.zeros((tq, tk), F32)
        for j in range(N_IDX_HEADS // 2):
            qp = qi_ref[0, j]
            w0 = wv[:, IDX_DIM + 2 * j:IDX_DIM + 2 * j + 1]
            w1 = wv[:, IDX_DIM + 2 * j + 1:IDX_DIM + 2 * j + 2]
            score = score + w0 * jnp.maximum(_nt(qp, k_lo), 0.0)
            score = score + w1 * jnp.maximum(_nt(qp, k_hi), 0.0)
        col = col0 + lax.broadcasted_iota(I32, (tq, tk), 1)
        keys_ref[:, pl.ds(col0, tk)] = jnp.where(col <= row, _sortable(score), INT_MIN)
        return carry

    lax.fori_loop(0, nblk, score_body, 0)
    nblk_a = (qblk * tq + tq + ta - 1) // ta

    def pad_body(kb, carry):
        keys_ref[:, pl.ds(pl.multiple_of(kb * tk, tk), tk)] = jnp.full((tq, tk), INT_MIN, I32)
        return carry

    lax.fori_loop(nblk, nblk_a * (ta // tk), pad_body, 0)
    _topk_threshold(keys_ref, thr_ref, jmax_ref, nblk, tk, topk)
    thr, jmax = thr_ref[...], jmax_ref[...]

    m_ref[...] = jnp.full(m_ref.shape, NEG_BIG, F32)
    l_ref[...] = jnp.zeros(l_ref.shape, F32)
    acc_ref[...] = jnp.zeros(acc_ref.shape, F32)

    def attn_body(kb, carry):
        col0 = pl.multiple_of(kb * ta, ta)
        bias = _select_bias(keys_ref, slice(None), col0, ta, thr, jmax)
        kblk = ka_ref[0, pl.ds(col0, ta), :]
        vblk = va_ref[0, pl.ds(col0, ta), :]
        for hh in range(N_HEADS_A):
            s = _nt(qa_ref[0, hh], kblk) + bias
            m_old = m_ref[hh]
            m_new = jnp.maximum(m_old, jnp.max(s, axis=1, keepdims=True))
            m_safe = jnp.where(m_new < 0.5 * NEG_BIG, 0.0, m_new)
            p = jnp.exp2(s - m_safe[:, :1])
            alpha = jnp.exp2(m_old - m_new)
            l_ref[hh] = alpha * l_ref[hh] + jnp.sum(p, axis=1, keepdims=True)
            acc_ref[hh] = alpha * acc_ref[hh] + jnp.dot(p.astype(BF16), vblk, preferred_element_type=F32)
            m_ref[hh] = m_new
        return carry

    lax.fori_loop(0, nblk_a, attn_body, 0)
    o_ref[0] = _head_out_pairs([acc_ref[hh] / l_ref[hh] for hh in range(N_HEADS_A)]).astype(o_ref.dtype)


def _prompt_group_a(qi, wi, qa, ki2, kab, vab):
    b, npair, s, _ = qi.shape
    tq = min(256, s)
    tk = min(512, s)
    ta = min(1024, s)
    topk = min(TOPK_MAX, s // 4)
    kern = functools.partial(_prompt_a_kernel, tq=tq, tk=tk, ta=ta, topk=topk)
    return pl.pallas_call(
        kern,
        grid=(b, s // tq),
        in_specs=[pl.BlockSpec((1, npair, tq, LANES), lambda bb, i: (bb, 0, i, 0)),
                  pl.BlockSpec((1, tq, LANES), lambda bb, i: (bb, i, 0)),
                  pl.BlockSpec((1, N_HEADS_A, tq, LANES), lambda bb, i: (bb, 0, i, 0)),
                  pl.BlockSpec((1, s, 2 * LANES), lambda bb, i: (bb, 0, 0)),
                  pl.BlockSpec((1, s, LANES), lambda bb, i: (bb, 0, 0)),
                  pl.BlockSpec((1, s, LANES), lambda bb, i: (bb, 0, 0))],
        out_specs=pl.BlockSpec((1, tq, N_HEADS_A * HEAD_DIM), lambda bb, i: (bb, i, 0)),
        out_shape=jax.ShapeDtypeStruct((b, s, N_HEADS_A * HEAD_DIM), BF16),
        scratch_shapes=[pltpu.VMEM((tq, s), I32),
                        pltpu.VMEM((tq, LANES), I32),
                        pltpu.VMEM((tq, LANES), I32),
                        pltpu.VMEM((N_HEADS_A, tq, LANES), F32),
                        pltpu.VMEM((N_HEADS_A, tq, LANES), F32),
                        pltpu.VMEM((N_HEADS_A, tq, LANES), F32)],
        compiler_params=_params(("arbitrary", "arbitrary")),
        name="prompt_group_a",
    )(qi, wi, qa, ki2, kab, vab)


def _log_sigmoid(z):
    return jnp.minimum(z, 0.0) - jnp.log(1.0 + jnp.exp(-jnp.abs(z)))


def _split3(x):
    hi = x.astype(BF16)
    r1 = x - hi.astype(F32)
    mid = r1.astype(BF16)
    lo = (r1 - mid.astype(F32)).astype(BF16)
    return hi, mid, lo


def _prompt_b_kernel(q_ref, k_ref, v_ref, o_ref, acc_ref, *, tq, tk):
    qblk = pl.program_id(2)
    q = q_ref[0]
    lane_q = lax.broadcasted_iota(I32, q.shape, 1)
    zero = jnp.zeros_like(q)
    qm = jnp.concatenate([jnp.where(lane_q < HEAD_DIM, q, zero), jnp.where(lane_q < HEAD_DIM, zero, q)], axis=0)
    row = qblk * tq + lax.broadcasted_iota(I32, (2 * tq, tk), 0) % tq
    jj = lax.broadcasted_iota(I32, (tk, tk), 0)
    ss = lax.broadcasted_iota(I32, (tk, tk), 1)
    tri = jnp.where(jj > ss, 1.0, 0.0).astype(BF16)
    kb0 = (qblk * tq + tq - 1) // tk
    acc_ref[...] = jnp.zeros(acc_ref.shape, F32)

    def cond(carry):
        kb, _, cmax = carry
        return (kb >= 0) & (cmax > SB_EXIT)

    def body(carry):
        kb, c, _ = carry
        col0 = pl.multiple_of(kb * tk, tk)
        z = _nt(qm, k_ref[0, pl.ds(col0, tk), :])
        valid = (col0 + lax.broadcasted_iota(I32, (2 * tq, tk), 1)) < row
        ls = _log_sigmoid(z)
        lk = jnp.where(valid, ls - z, 0.0)
        hi, mid, lo = _split3(lk)
        after = (jnp.dot(hi, tri, preferred_element_type=F32)
                 + jnp.dot(mid, tri, preferred_element_type=F32)
                 + jnp.dot(lo, tri, preferred_element_type=F32))
        a = jnp.where(valid, jnp.exp(ls + after + c), 0.0)
        acc_ref[...] += jnp.dot(a.astype(BF16), v_ref[0, pl.ds(col0, tk), :], preferred_element_type=F32)
        c = c + jnp.sum(lk, axis=1, keepdims=True)
        return kb - 1, c, jnp.max(c)

    lax.while_loop(cond, body, (kb0, jnp.zeros((2 * tq, 1), F32), jnp.float32(0.0)))

    lane = lax.broadcasted_iota(I32, (tq, LANES), 1)
    o_ref[0] = jnp.where(lane < HEAD_DIM, acc_ref[:tq], acc_ref[tq:]).astype(o_ref.dtype)


def _prompt_group_b(qb, kbb, vbb):
    b, s, _ = qb.shape
    tq = min(256, s)
    tk = min(256, s)
    npair = N_HEADS_B // 2
    kern = functools.partial(_prompt_b_kernel, tq=tq, tk=tk)
    return pl.pallas_call(
        kern,
        grid=(b, npair, s // tq),
        in_specs=[pl.BlockSpec((1, tq, LANES), lambda bb, j, i: (bb, i, j)),
                  pl.BlockSpec((1, s, LANES), lambda bb, j, i: (bb, 0, j)),
                  pl.BlockSpec((1, s, LANES), lambda bb, j, i: (bb, 0, j))],
        out_specs=pl.BlockSpec((1, tq, LANES), lambda bb, j, i: (bb, i, j)),
        out_shape=jax.ShapeDtypeStruct((b, s, N_HEADS_B * HEAD_DIM), BF16),
        scratch_shapes=[pltpu.VMEM((2 * tq, LANES), F32)],
        compiler_params=_params(("arbitrary", "arbitrary", "arbitrary")),
        name="prompt_group_b",
    )(qb, kbb, vbb)


def _sample_a_kernel(pt_ref, qi_ref, w_ref, qa_ref, kin_ref, kan_ref, van_ref,
                     cki_ref, cka_ref, cva_ref, o_ref,
                     ki_buf, ka_buf, va_buf, sems, keys_ref, thr_ref, jmax_ref, s_ref,
                     *, n_pages, n_new, topk, chunk):
    b = pl.program_id(0)
    past = n_pages * PAGE_SIZE
    tpad = keys_ref.shape[0]
    rows = qa_ref.shape[1]
    slot = b % 2

    def page_copies(seq, buf_slot, p):
        phys = pt_ref[seq, p]
        dst = pl.ds(pl.multiple_of(p * PAGE_SIZE, PAGE_SIZE), PAGE_SIZE)
        return (pltpu.make_async_copy(cki_ref.at[phys], ki_buf.at[buf_slot, :, dst], sems.at[buf_slot, 0]),
                pltpu.make_async_copy(cka_ref.at[phys], ka_buf.at[buf_slot, :, dst], sems.at[buf_slot, 1]),
                pltpu.make_async_copy(cva_ref.at[phys], va_buf.at[buf_slot, :, dst], sems.at[buf_slot, 2]))

    def start_seq(seq, buf_slot):
        def body(p, carry):
            for cp in page_copies(seq, buf_slot, p):
                cp.start()
            return carry

        lax.fori_loop(0, n_pages, body, 0)

    def wait_seq(seq, buf_slot):
        def body(p, carry):
            for cp in page_copies(seq, buf_slot, p):
                cp.wait()
            return carry

        lax.fori_loop(0, n_pages, body, 0)

    @pl.when(b == 0)
    def _():
        start_seq(0, 0)

    @pl.when(b + 1 < pl.num_programs(0))
    def _():
        start_seq(b + 1, 1 - slot)

    wait_seq(b, slot)

    qi = qi_ref[0]
    w = w_ref[0] * IDX_SCALE
    tok = lax.broadcasted_iota(I32, (tpad, LANES), 0)

    def scores_of(kt):
        d = jnp.maximum(jnp.dot(qi, kt, preferred_element_type=F32), 0.0) * w
        return jnp.sum(d.reshape(tpad, N_IDX_HEADS, kt.shape[1]), axis=1)

    def score_body(c, carry):
        col0 = pl.multiple_of(c * chunk, chunk)
        sc = scores_of(ki_buf[slot, :, pl.ds(col0, chunk)].astype(BF16))
        keys_ref[:, pl.ds(col0, chunk)] = _sortable(sc)
        return carry

    lax.fori_loop(0, past // chunk, score_body, 0)
    new_col = lax.broadcasted_iota(I32, (tpad, LANES), 1)
    new_ok = (new_col <= tok) & (new_col < n_new)
    keys_ref[:, past:past + LANES] = jnp.where(new_ok, _sortable(scores_of(kin_ref[0])), INT_MIN)

    width = past + LANES
    sel_tk = max(t for t in range(LANES, 2048 + 1, LANES) if width % t == 0)
    _topk_threshold(keys_ref, thr_ref, jmax_ref, width // sel_tk, sel_tk, topk, row_ok=tok < n_new)
    thr, jmax = thr_ref[...], jmax_ref[...]

    qa = qa_ref[0]

    def masked_scores(kt, col0):
        n = kt.shape[1]
        bias = _select_bias(keys_ref, slice(None), col0, n, thr, jmax)
        s = jnp.dot(qa, kt, preferred_element_type=F32).reshape(tpad, N_HEADS_A, n) + bias[:, None, :]
        return s.reshape(rows, n)

    def s_body(c, m):
        col0 = pl.multiple_of(c * chunk, chunk)
        s = masked_scores(ka_buf[slot, :, pl.ds(col0, chunk)].astype(BF16), col0)
        s_ref[:, pl.ds(col0, chunk)] = s
        return jnp.maximum(m, jnp.max(s, axis=1, keepdims=True))

    m = lax.fori_loop(0, past // chunk, s_body, jnp.full((rows, 1), NEG_BIG, F32))
    s_new = masked_scores(kan_ref[0], past)
    m = jnp.maximum(m, jnp.max(s_new, axis=1, keepdims=True))

    p_new = jnp.exp2(s_new - m)
    l0 = jnp.sum(p_new, axis=1, keepdims=True)
    acc0 = _nt(p_new.astype(BF16), van_ref[0])

    def pv_body(c, carry):
        l, acc = carry
        col0 = pl.multiple_of(c * chunk, chunk)
        p = jnp.exp2(s_ref[:, pl.ds(col0, chunk)] - m)
        acc = acc + _nt(p.astype(BF16), va_buf[slot, :, pl.ds(col0, chunk)].astype(BF16))
        return l + jnp.sum(p, axis=1, keepdims=True), acc

    l, acc = lax.fori_loop(0, past // chunk, pv_body, (l0, acc0))
    o_ref[0] = acc / l


def _sample_group_a(page_table, qi_r, w_r, qa_r, kin, kan, van, cki, cka, cva, n_new):
    db, n_pages = page_table.shape
    past = n_pages * PAGE_SIZE
    rows = qa_r.shape[1]
    tpad = rows // N_HEADS_A
    topk = min(TOPK_MAX, (past + n_new) // 4)
    chunk = min(1024, past)
    kern = functools.partial(_sample_a_kernel, n_pages=n_pages, n_new=n_new, topk=topk, chunk=chunk)
    grid_spec = pltpu.PrefetchScalarGridSpec(
        num_scalar_prefetch=1,
        grid=(db,),
        in_specs=[pl.BlockSpec((1, rows, IDX_DIM), lambda i, pt: (i, 0, 0)),
                  pl.BlockSpec((1, rows, 1), lambda i, pt: (i, 0, 0)),
                  pl.BlockSpec((1, rows, LANES), lambda i, pt: (i, 0, 0)),
                  pl.BlockSpec((1, IDX_DIM, LANES), lambda i, pt: (i, 0, 0)),
                  pl.BlockSpec((1, LANES, LANES), lambda i, pt: (i, 0, 0)),
                  pl.BlockSpec((1, LANES, LANES), lambda i, pt: (i, 0, 0)),
                  pl.BlockSpec(memory_space=pl.ANY),
                  pl.BlockSpec(memory_space=pl.ANY),
                  pl.BlockSpec(memory_space=pl.ANY)],
        out_specs=pl.BlockSpec((1, rows, LANES), lambda i, pt: (i, 0, 0)),
        scratch_shapes=[pltpu.VMEM((2, IDX_DIM, past), F32),
                        pltpu.VMEM((2, LANES, past), F32),
                        pltpu.VMEM((2, LANES, past), F32),
                        pltpu.SemaphoreType.DMA((2, 3)),
                        pltpu.VMEM((tpad, past + LANES), I32),
                        pltpu.VMEM((tpad, LANES), I32),
                        pltpu.VMEM((tpad, LANES), I32),
                        pltpu.VMEM((rows, past), F32)])
    return pl.pallas_call(
        kern,
        grid_spec=grid_spec,
        out_shape=jax.ShapeDtypeStruct((db, rows, LANES), F32),
        compiler_params=_params(("arbitrary",)),
        name="sample_group_a",
    )(page_table, qi_r, w_r, qa_r, kin, kan, van, cki, cka, cva)


def _sample_b_kernel(pt_ref, q_ref, kn_ref, vn_ref, ck_ref, cv_ref, o_ref,
                     k_buf, v_buf, sems, *, n_pages):
    b = pl.program_id(0)
    q = q_ref[0]
    rows = q.shape[0]
    jj = lax.broadcasted_iota(I32, (PAGE_SIZE, PAGE_SIZE), 0)
    ss = lax.broadcasted_iota(I32, (PAGE_SIZE, PAGE_SIZE), 1)
    tri = jnp.where(jj > ss, 1.0, 0.0).astype(BF16)
    last = n_pages - 1

    def copies(p, slot):
        phys = pt_ref[b, p]
        return (pltpu.make_async_copy(ck_ref.at[phys], k_buf.at[slot], sems.at[slot, 0]),
                pltpu.make_async_copy(cv_ref.at[phys], v_buf.at[slot], sems.at[slot, 1]))

    def block(kt, vt, valid, c):
        z = jnp.dot(q, kt, preferred_element_type=F32)
        ls = _log_sigmoid(z)
        lk = ls - z
        if valid is not None:
            lk = jnp.where(valid, lk, 0.0)
        hi, mid, lo = _split3(lk)
        after = (jnp.dot(hi, tri, preferred_element_type=F32)
                 + jnp.dot(mid, tri, preferred_element_type=F32)
                 + jnp.dot(lo, tri, preferred_element_type=F32))
        a = jnp.exp(ls + after + c)
        if valid is not None:
            a = jnp.where(valid, a, 0.0)
        return c + jnp.sum(lk, axis=1, keepdims=True), _nt(a.astype(BF16), vt)

    for cp in copies(last, 0):
        cp.start()

    tok = lax.broadcasted_iota(I32, (rows, PAGE_SIZE), 0) // N_HEADS_B
    valid_new = lax.broadcasted_iota(I32, (rows, PAGE_SIZE), 1) < tok
    c, o = block(kn_ref[0], vn_ref[0], valid_new, jnp.zeros((rows, 1), F32))

    def cond(carry):
        p, _, _, cmax = carry
        return (p >= 0) & (cmax > SB_EXIT)

    def body(carry):
        p, c, o, _ = carry
        slot = (last - p) % 2
        for cp in copies(p, slot):
            cp.wait()

        @pl.when(p > 0)
        def _():
            for cp in copies(p - 1, 1 - slot):
                cp.start()

        c, o_add = block(k_buf[slot].astype(BF16), v_buf[slot].astype(BF16), None, c)
        return p - 1, c, o + o_add, jnp.max(c)

    p_end, _, o, _ = lax.while_loop(cond, body, (jnp.int32(last), c, o, jnp.max(c)))

    @pl.when(p_end >= 0)
    def _():
        for cp in copies(p_end, (last - p_end) % 2):
            cp.wait()

    o_ref[0] = o


def _sample_group_b(page_table, q_bd, knt, vnt, ckb, cvb):
    db, n_pages = page_table.shape
    rows, width = q_bd.shape[1:]
    kern = functools.partial(_sample_b_kernel, n_pages=n_pages)
    grid_spec = pltpu.PrefetchScalarGridSpec(
        num_scalar_prefetch=1,
        grid=(db,),
        in_specs=[pl.BlockSpec((1, rows, width), lambda i, pt: (i, 0, 0)),
                  pl.BlockSpec((1, width, PAGE_SIZE), lambda i, pt: (i, 0, 0)),
                  pl.BlockSpec((1, width, PAGE_SIZE), lambda i, pt: (i, 0, 0)),
                  pl.BlockSpec(memory_space=pl.ANY),
                  pl.BlockSpec(memory_space=pl.ANY)],
        out_specs=pl.BlockSpec((1, rows, width), lambda i, pt: (i, 0, 0)),
        scratch_shapes=[pltpu.VMEM((2, width, PAGE_SIZE), F32),
                        pltpu.VMEM((2, width, PAGE_SIZE), F32),
                        pltpu.SemaphoreType.DMA((2, 2))])
    return pl.pallas_call(
        kern,
        grid_spec=grid_spec,
        out_shape=jax.ShapeDtypeStruct((db, rows, width), F32),
        compiler_params=_params(("arbitrary",)),
        name="sample_group_b",
    )(page_table, q_bd, knt, vnt, ckb, cvb)


def _layernorm(x, g, b):
    mu = jnp.mean(x, axis=-1, keepdims=True)
    xc = x - mu
    var = jnp.mean(xc * xc, axis=-1, keepdims=True)
    return xc * lax.rsqrt(var + LN_EPS) * g + b


def _attn_out_kernel(x_ref, oa_ref, ob_ref, gate_ref, wo_ref, g_ref, b_ref, o_ref, *, alpha):
    half = oa_ref.shape[2]
    mix = (jnp.dot(oa_ref[0], wo_ref[:half, :], preferred_element_type=F32)
           + jnp.dot(ob_ref[0], wo_ref[half:, :], preferred_element_type=F32))
    o_ref[0] = _layernorm(alpha * x_ref[0] + gate_ref[0] * mix, g_ref[...], b_ref[...])


def _attn_out(x, oa, ob, gate, w_o, g, bta, alpha):
    gg, r, d = x.shape
    tm = min(512, r)
    rm = gate.shape[1]
    mod_blk = (1, 1, d) if rm == 1 else (1, tm, d)
    mod_map = (lambda b, i: (b, 0, 0)) if rm == 1 else (lambda b, i: (b, i, 0))
    wa = oa.shape[2]
    return pl.pallas_call(
        functools.partial(_attn_out_kernel, alpha=alpha),
        grid=(gg, r // tm),
        in_specs=[pl.BlockSpec((1, tm, d), lambda b, i: (b, i, 0)),
                  pl.BlockSpec((1, tm, wa), lambda b, i: (b, i, 0)),
                  pl.BlockSpec((1, tm, wa), lambda b, i: (b, i, 0)),
                  pl.BlockSpec(mod_blk, mod_map),
                  pl.BlockSpec((2 * wa, d), lambda b, i: (0, 0)),
                  pl.BlockSpec((1, d), lambda b, i: (0, 0)),
                  pl.BlockSpec((1, d), lambda b, i: (0, 0))],
        out_specs=pl.BlockSpec((1, tm, d), lambda b, i: (b, i, 0)),
        out_shape=jax.ShapeDtypeStruct((gg, r, d), F32),
        compiler_params=_params(("arbitrary", "arbitrary")),
        name="attn_out_ln",
    )(x, oa, ob, gate, w_o, g.reshape(1, d), bta.reshape(1, d))


def _ffn_kernel(x_ref, shift_ref, scale_ref, gate_ref, wu_ref, wd_ref, g_ref, b_ref, o_ref,
                h_ref, acc_ref, *, alpha):
    f = pl.program_id(2)

    @pl.when(f == 0)
    def _():
        h_ref[...] = (x_ref[0] * (1.0 + scale_ref[0]) + shift_ref[0]).astype(BF16)
        acc_ref[...] = jnp.zeros(acc_ref.shape, F32)

    u = jnp.maximum(jnp.dot(h_ref[...], wu_ref[...], preferred_element_type=F32), 0.0)
    acc_ref[...] += jnp.dot((u * u).astype(BF16), wd_ref[...], preferred_element_type=F32)

    @pl.when(f == pl.num_programs(2) - 1)
    def _():
        o_ref[0] = _layernorm(alpha * x_ref[0] + gate_ref[0] * acc_ref[...], g_ref[...], b_ref[...])


def _ffn(x, shift, scale, gate, w_up, w_down, g, bta, alpha):
    gg, r, d = x.shape
    dff = w_up.shape[1]
    tm = min(512, r)
    tf = min(1024, dff)
    rm = gate.shape[1]
    mod_blk = (1, 1, d) if rm == 1 else (1, tm, d)
    mod_map = (lambda b, i, f: (b, 0, 0)) if rm == 1 else (lambda b, i, f: (b, i, 0))
    return pl.pallas_call(
        functools.partial(_ffn_kernel, alpha=alpha),
        grid=(gg, r // tm, dff // tf),
        in_specs=[pl.BlockSpec((1, tm, d), lambda b, i, f: (b, i, 0)),
                  pl.BlockSpec(mod_blk, mod_map),
                  pl.BlockSpec(mod_blk, mod_map),
                  pl.BlockSpec(mod_blk, mod_map),
                  pl.BlockSpec((d, tf), lambda b, i, f: (0, f)),
                  pl.BlockSpec((tf, d), lambda b, i, f: (f, 0)),
                  pl.BlockSpec((1, d), lambda b, i, f: (0, 0)),
                  pl.BlockSpec((1, d), lambda b, i, f: (0, 0))],
        out_specs=pl.BlockSpec((1, tm, d), lambda b, i, f: (b, i, 0)),
        out_shape=jax.ShapeDtypeStruct((gg, r, d), F32),
        scratch_shapes=[pltpu.VMEM((tm, d), BF16), pltpu.VMEM((tm, d), F32)],
        compiler_params=_params(("arbitrary", "arbitrary", "arbitrary")),
        name="ffn_ln",
    )(x, shift, scale, gate, w_up, w_down, g.reshape(1, d), bta.reshape(1, d))


def _rope_tables(pos):
    inv = 1.0 / (ROPE_THETA ** (jnp.arange(HALF, dtype=F32) * (2.0 / HEAD_DIM)))
    ang = pos.astype(F32)[:, None] * inv[None, :]
    cos = jnp.tile(jnp.cos(ang), (1, 2 * LANES // HEAD_DIM))
    sin = jnp.sin(ang)
    sin = jnp.tile(jnp.concatenate([-sin, sin], axis=1), (1, LANES // HEAD_DIM))
    return cos, sin


def _pad_w_in(w_in_l):
    d = w_in_l.shape[0]
    group = N_HEADS_A // N_KV_A
    o_qa, o_ka, o_va = 0, 512, 640
    o_qi, o_ki, o_wi, o_qb, o_kb, o_vb = 768, 1280, 1344, 1352, 1864, 2376
    cols = []
    zero = jnp.zeros((d, HEAD_DIM), w_in_l.dtype)
    for hh in range(N_HEADS_A):
        wq = w_in_l[:, o_qa + hh * HEAD_DIM:o_qa + (hh + 1) * HEAD_DIM]
        cols += [wq, zero] if hh // group == 0 else [zero, wq]
    cols.append(w_in_l[:, o_ka:o_qi])
    cols.append(w_in_l[:, o_qi:o_ki])
    cols.append(w_in_l[:, o_ki:o_qb])
    cols.append(jnp.zeros((d, LANES - IDX_DIM - N_IDX_HEADS), w_in_l.dtype))
    cols.append(w_in_l[:, o_qb:])
    w = jnp.concatenate(cols, axis=1)
    assert w.shape[1] == _W_TOT
    return w.astype(BF16)


def kernel(x_prompt, x_sample, cache_k_a, cache_v_a, cache_k_idx, cache_k_b, cache_v_b, page_table,
           c_prompt, c_sample, w_mod, b_mod, w_in, w_o, ln1_g, ln1_b, w_up, w_down, ln2_g, ln2_b):
    depth = w_in.shape[0]
    alpha = float((2.0 * depth) ** 0.25)
    bsz, seq, d = x_prompt.shape
    db, t_new, _ = x_sample.shape
    n_pages = page_table.shape[1]
    past = n_pages * PAGE_SIZE
    n_phys = cache_k_a.shape[1]
    tpad = 8

    def page_major(cache):
        nd = cache.ndim
        t = jnp.transpose(cache, (0, 1) + tuple(range(3, nd)) + (2,))
        return t.reshape(depth, n_phys, -1, PAGE_SIZE)

    cki_t, cka_t, cva_t = page_major(cache_k_idx), page_major(cache_k_a), page_major(cache_v_a)
    ckb_t, cvb_t = page_major(cache_k_b), page_major(cache_v_b)

    cos_p, sin_p = _rope_tables(jnp.arange(seq, dtype=I32))
    cos_s, sin_s = _rope_tables(past + jnp.arange(t_new, dtype=I32))
    cos_s = jnp.tile(cos_s, (db, 1))
    sin_s = jnp.tile(sin_s, (db, 1))

    yp = x_prompt
    ys = x_sample.reshape(1, db * t_new, d)
    rows_p, rows_s = [], []
    for l in range(depth):
        w_pad = _pad_w_in(w_in[l])
        wo_b, wu_b, wd_b = w_o[l].astype(BF16), w_up[l].astype(BF16), w_down[l].astype(BF16)

        mod_p = _modulation(c_prompt, w_mod[l], b_mod[l])[:, None, :]
        sh_m, sc_m, g_m, sh_f, sc_f, g_f = jnp.split(mod_p, 6, axis=-1)
        (qa, ka, va, qi, ki, wi, qb, kb, vb, kab, vab, ki2, kbb, vbb) = _project(
            yp, sh_m, sc_m, cos_p, sin_p, w_pad)
        oa = _prompt_group_a(qi, wi, qa, ki2, kab, vab)
        ob = _prompt_group_b(qb, kbb, vbb)
        x1 = _attn_out(yp, oa, ob, g_m, wo_b, ln1_g[l], ln1_b[l], alpha)
        yp = _ffn(x1, sh_f, sc_f, g_f, wu_b, wd_b, ln2_g[l], ln2_b[l], alpha)
        rows_p.append((ka.reshape(bsz, seq, N_KV_A, HEAD_DIM), va.reshape(bsz, seq, N_KV_A, HEAD_DIM), ki,
                       kb.reshape(bsz, seq, N_HEADS_B, HEAD_DIM), vb.reshape(bsz, seq, N_HEADS_B, HEAD_DIM)))

        mod_s = _modulation(c_sample, w_mod[l], b_mod[l])
        mod_s = jnp.broadcast_to(mod_s[:, None, :], (db, t_new, 6 * d)).reshape(1, db * t_new, 6 * d)
        sh_m, sc_m, g_m, sh_f, sc_f, g_f = jnp.split(mod_s, 6, axis=-1)
        (qa, ka, va, qi, ki, wi, qb, kb, vb, kab, vab, ki2, kbb, vbb) = _project(
            ys, sh_m, sc_m, cos_s, sin_s, w_pad)

        def pad_tokens(a, n):
            return jnp.pad(a, [(0, 0), (0, n - a.shape[1])] + [(0, 0)] * (a.ndim - 2))

        qi_r = qi[0].reshape(N_IDX_HEADS // 2, db, t_new, 2, IDX_DIM).transpose(1, 2, 0, 3, 4)
        qi_r = pad_tokens(qi_r.reshape(db, t_new, N_IDX_HEADS, IDX_DIM), tpad)
        qi_r = qi_r.reshape(db, tpad * N_IDX_HEADS, IDX_DIM)
        w_r = pad_tokens(wi.reshape(db, t_new, LANES)[:, :, IDX_DIM:IDX_DIM + N_IDX_HEADS], tpad)
        w_r = w_r.reshape(db, tpad * N_IDX_HEADS, 1)
        qa_r = qa[0].reshape(N_HEADS_A, db, t_new, LANES).transpose(1, 2, 0, 3)
        qa_r = pad_tokens(qa_r, tpad).reshape(db, tpad * N_HEADS_A, LANES)

        def new_page(a):
            a = jnp.swapaxes(a.reshape(db, t_new, a.shape[-1]), 1, 2)
            return jnp.pad(a, [(0, 0), (0, 0), (0, PAGE_SIZE - t_new)]).astype(BF16)

        oa_r = _sample_group_a(page_table, qi_r, w_r, qa_r, new_page(ki), new_page(kab), new_page(vab),
                               cki_t[l], cka_t[l], cva_t[l], t_new)
        oa_r = oa_r.reshape(db, tpad, N_KV_A, N_HEADS_A // N_KV_A, N_KV_A, HEAD_DIM)[:, :t_new]
        oa_s = jnp.stack([oa_r[:, :, n, :, n, :] for n in range(N_KV_A)], axis=2)
        oa_s = oa_s.reshape(1, db * t_new, N_HEADS_A * HEAD_DIM).astype(BF16)

        eye = jnp.eye(N_HEADS_B, dtype=BF16)
        q_bd = jnp.einsum('bthd,gh->btghd', qb.reshape(db, t_new, N_HEADS_B, HEAD_DIM), eye)
        q_bd = q_bd.reshape(db, t_new * N_HEADS_B, N_HEADS_B * HEAD_DIM)
        ob_r = _sample_group_b(page_table, q_bd, new_page(kbb), new_page(vbb), ckb_t[l], cvb_t[l])
        ob_r = ob_r.reshape(db, t_new, N_HEADS_B, N_HEADS_B, HEAD_DIM)
        ob_s = jnp.stack([ob_r[:, :, hh, hh, :] for hh in range(N_HEADS_B)], axis=2)
        ob_s = ob_s.reshape(1, db * t_new, N_HEADS_B * HEAD_DIM).astype(BF16)

        x1 = _attn_out(ys, oa_s, ob_s, g_m, wo_b, ln1_g[l], ln1_b[l], alpha)
        ys = _ffn(x1, sh_f, sc_f, g_f, wu_b, wd_b, ln2_g[l], ln2_b[l], alpha)
        rows_s.append((ka.reshape(db, t_new, N_KV_A, HEAD_DIM), va.reshape(db, t_new, N_KV_A, HEAD_DIM),
                       ki.reshape(db, t_new, IDX_DIM),
                       kb.reshape(db, t_new, N_HEADS_B, HEAD_DIM), vb.reshape(db, t_new, N_HEADS_B, HEAD_DIM)))

    outs_p = [jnp.stack([r[i] for r in rows_p]) for i in range(5)]
    outs_s = [jnp.stack([r[i] for r in rows_s]) for i in range(5)]
    return (yp, ys.reshape(db, t_new, d), *outs_p, *outs_s)
```
